```python
import jax, jax.numpy as jnp
from jax import lax
import numpy as np

D_MODEL = 2048
BATCH = 8
SEQ = 4096
DEPTH = 4
DEC_BATCH = 8
DEC_SEQ = 32
PAST_LEN = 1024

CHUNK = 64
HEAD_DIM = 128
H_A = 8
H_A_KV = 2
WINDOW = 128
WIN_CHUNKS = WINDOW // CHUNK
H_B = 4
BAND_CHUNKS = 8
BAND_ROWS = BAND_CHUNKS * CHUNK
REL_CLIP = 128
H_M = 4
N_MEM = 256
MIX_WIDTH = (H_A + H_B + H_M) * HEAD_DIM
IN_COLS = (H_A + 2 * H_A_KV + 3 * H_B + H_M) * HEAD_DIM
ROT_DIM = HEAD_DIM // 4
ROPE_THETA = 500000.0
D_FF = ((8 * D_MODEL // 3 + 127) // 128) * 128
EPS = 1e-6

kernel_name = 'hybrid_streaming_encoder_step'


def rms_norm(x, g):
    xf = x.astype(jnp.float32)
    y = xf * lax.rsqrt(jnp.mean(xf * xf, axis=-1, keepdims=True) + EPS)
    return (y * g.astype(jnp.float32)).astype(x.dtype)


def swiglu(h, w_i, w_o):
    gate, up = jnp.split(h @ w_i, 2, axis=-1)
    return (jax.nn.silu(gate) * up) @ w_o


def rope_partial(x, pos):
    half = ROT_DIM // 2
    inv_freq = ROPE_THETA ** (-jnp.arange(half, dtype=jnp.float32) / half)
    ang = pos.astype(jnp.float32)[:, None] * inv_freq[None, :]
    cos = jnp.cos(ang)[None, :, None, :]
    sin = jnp.sin(ang)[None, :, None, :]
    xf = x.astype(jnp.float32)
    x1, x2, rest = xf[..., :half], xf[..., half:ROT_DIM], xf[..., ROT_DIM:]
    out = jnp.concatenate([x1 * cos - x2 * sin, x2 * cos + x1 * sin, rest], axis=-1)
    return out.astype(x.dtype)


def rel_bias_matrix(table):
    band = BAND_ROWS + CHUNK
    i = jnp.arange(CHUNK)[:, None]
    j = jnp.arange(band)[None, :]
    dist = BAND_ROWS + i - j
    idx = jnp.clip(dist, -REL_CLIP, REL_CLIP) + REL_CLIP
    return table[:, idx].astype(jnp.float32)[:, None]


def chunk_band_attention(q, k, v, k_past, v_past, n_past_chunks, sinks, bias):
    b, t, hq, d = q.shape
    hk = k.shape[2]
    g = hq // hk
    p_rows = n_past_chunks * CHUNK
    lp = k_past.shape[1]
    n_chunks = -(-t // CHUNK)
    tp = n_chunks * CHUNK
    band = p_rows + CHUNK

    def pad_rows(x_past, x_new):
        zl = jnp.zeros((b, p_rows - lp, hk, d), x_new.dtype)
        zr = jnp.zeros((b, tp - t, hk, d), x_new.dtype)
        return jnp.concatenate([zl, x_past.astype(x_new.dtype), x_new, zr], axis=1)

    kk = pad_rows(k_past, k)
    vv = pad_rows(v_past, v)
    idx = jnp.arange(p_rows + tp)
    valid = (idx >= p_rows - lp) & (idx < p_rows + t)
    qq = jnp.pad(q, ((0, 0), (0, tp - t), (0, 0), (0, 0))).reshape(b, n_chunks, CHUNK, hk, g, d)
    qq = jnp.moveaxis(qq, 1, 0)
    scale = d ** -0.5
    if sinks is not None:
        sink = sinks.astype(jnp.float32).reshape(hk, g)[None, :, :, None, None]

    def one_chunk(args):
        c, qb = args
        start = c * CHUNK
        kb = lax.dynamic_slice_in_dim(kk, start, band, axis=1)
        vb = lax.dynamic_slice_in_dim(vv, start, band, axis=1)
        mb = lax.dynamic_slice_in_dim(valid, start, band)
        s = jnp.einsum('bqhgd,bkhd->bhgqk', qb.astype(jnp.float32), kb.astype(jnp.float32)) * scale
        if bias is not None:
            s = s + bias[None]
        s = jnp.where(mb[None, None, None, None, :], s, -jnp.inf)
        if sinks is not None:
            m = jnp.maximum(jnp.max(s, axis=-1, keepdims=True), sink)
            e = jnp.exp(s - m)
            p = e / (jnp.sum(e, axis=-1, keepdims=True) + jnp.exp(sink - m))
        else:
            p = jax.nn.softmax(s, axis=-1)
        return jnp.einsum('bhgqk,bkhd->bqhgd', p.astype(vb.dtype), vb)

    out = lax.map(one_chunk, (jnp.arange(n_chunks), qq))
    out = jnp.moveaxis(out, 0, 1).reshape(b, tp, hq, d)
    return out[:, :t]


def memory_kv(mem, g_norm, w_kv, g_k):
    b, n, _ = mem.shape
    m = rms_norm(mem, g_norm) @ w_kv
    mk, mv = jnp.split(m, 2, axis=-1)
    mk = rms_norm(mk.reshape(b, n, H_M, HEAD_DIM), g_k)
    return mk, mv.reshape(b, n, H_M, HEAD_DIM)


def memory_attention(q, mk, mv):
    s = jnp.einsum('bqhd,bmhd->bhqm', q.astype(jnp.float32), mk.astype(jnp.float32)) * (HEAD_DIM ** -0.5)
    p = jax.nn.softmax(s, axis=-1)
    return jnp.einsum('bhqm,bmhd->bqhd', p.astype(mv.dtype), mv.astype(q.dtype))


def trunk_layer(x, pos, a_k_past, a_v_past, b_k_past, b_v_past, mem_k, mem_v, lw):
    (g_ff1, w1i, w1o, g_mix, w_in_l, qk_g, sinks, rel_tab, w_out_l, g_ff2, w2i, w2o) = lw
    b, t, _ = x.shape
    x = x + 0.5 * swiglu(rms_norm(x, g_ff1), w1i, w1o)
    h = rms_norm(x, g_mix)
    proj = h @ w_in_l
    sizes = (H_A * HEAD_DIM, H_A_KV * HEAD_DIM, H_A_KV * HEAD_DIM,
             H_B * HEAD_DIM, H_B * HEAD_DIM, H_B * HEAD_DIM)
    points = []
    acc = 0
    for s_ in sizes:
        acc += s_
        points.append(acc)
    qa, ka, va, qb, kb, vb, qm = jnp.split(proj, points, axis=-1)

    def heads(z, n):
        return z.reshape(b, t, n, HEAD_DIM)

    qa = rope_partial(rms_norm(heads(qa, H_A), qk_g[0]), pos)
    ka = rope_partial(rms_norm(heads(ka, H_A_KV), qk_g[1]), pos)
    va = heads(va, H_A_KV)
    qb = rms_norm(heads(qb, H_B), qk_g[2])
    kb = rms_norm(heads(kb, H_B), qk_g[3])
    vb = heads(vb, H_B)
    qm = rms_norm(heads(qm, H_M), qk_g[4])

    o_a = chunk_band_attention(qa, ka, va, a_k_past, a_v_past, WIN_CHUNKS, sinks, None)
    o_b = chunk_band_attention(qb, kb, vb, b_k_past, b_v_past, BAND_CHUNKS, None, rel_bias_matrix(rel_tab))
    o_m = memory_attention(qm, mem_k, mem_v)
    o = jnp.concatenate([o_a.reshape(b, t, -1), o_b.reshape(b, t, -1), o_m.reshape(b, t, -1)], axis=-1)
    x = x + o @ w_out_l
    x = x + 0.5 * swiglu(rms_norm(x, g_ff2), w2i, w2o)
    return x, ka, va, kb, vb


def setup_inputs(seed: int = 0) -> dict:
    key = jax.random.key(seed)
    ks = jax.random.split(key, 32)
    f32 = jnp.float32
    la = min(WINDOW, PAST_LEN)
    lb = min(BAND_ROWS, PAST_LEN)

    def nrm(k, shape, scale=1.0):
        return jax.random.normal(k, shape, f32) * scale

    def gain(k, shape):
        return 1.0 + 0.05 * jax.random.normal(k, shape, f32)

    return {
        'x_prompt': nrm(ks[0], (BATCH, SEQ, D_MODEL)),
        'x_sample': nrm(ks[1], (DEC_BATCH, DEC_SEQ, D_MODEL)),
        'cache_a_k': nrm(ks[2], (DEPTH, DEC_BATCH, la, H_A_KV, HEAD_DIM)),
        'cache_a_v': nrm(ks[3], (DEPTH, DEC_BATCH, la, H_A_KV, HEAD_DIM)),
        'cache_b_k': nrm(ks[4], (DEPTH, DEC_BATCH, lb, H_B, HEAD_DIM)),
        'cache_b_v': nrm(ks[5], (DEPTH, DEC_BATCH, lb, H_B, HEAD_DIM)),
        'cache_mem_k': nrm(ks[6], (DEPTH, DEC_BATCH, N_MEM, H_M, HEAD_DIM)),
        'cache_mem_v': nrm(ks[7], (DEPTH, DEC_BATCH, N_MEM, H_M, HEAD_DIM)),
        'mem_prompt': nrm(ks[8], (BATCH, N_MEM, D_MODEL)),
        'norm_ff1': gain(ks[9], (DEPTH, D_MODEL)),
        'w_ff1_in': nrm(ks[10], (DEPTH, D_MODEL, 2 * D_FF), D_MODEL ** -0.5),
        'w_ff1_out': nrm(ks[11], (DEPTH, D_FF, D_MODEL), D_FF ** -0.5),
        'norm_mix': gain(ks[12], (DEPTH, D_MODEL)),
        'w_in': nrm(ks[13], (DEPTH, D_MODEL, IN_COLS), D_MODEL ** -0.5),
        'qk_gain': gain(ks[14], (DEPTH, 6, HEAD_DIM)),
        'attn_sinks': nrm(ks[15], (DEPTH, H_A), 0.5),
        'rel_bias': nrm(ks[16], (DEPTH, H_B, 2 * REL_CLIP + 1), 0.1),
        'norm_mem': gain(ks[17], (DEPTH, D_MODEL)),
        'w_mem_kv': nrm(ks[18], (DEPTH, D_MODEL, 2 * H_M * HEAD_DIM), D_MODEL ** -0.5),
        'w_out': nrm(ks[19], (DEPTH, MIX_WIDTH, D_MODEL), MIX_WIDTH ** -0.5),
        'norm_ff2': gain(ks[20], (DEPTH, D_MODEL)),
        'w_ff2_in': nrm(ks[21], (DEPTH, D_MODEL, 2 * D_FF), D_MODEL ** -0.5),
        'w_ff2_out': nrm(ks[22], (DEPTH, D_FF, D_MODEL), D_FF ** -0.5),
    }


def reference(x_prompt, x_sample, cache_a_k, cache_a_v, cache_b_k, cache_b_v, cache_mem_k, cache_mem_v,
              mem_prompt, norm_ff1, w_ff1_in, w_ff1_out, norm_mix, w_in, qk_gain, attn_sinks, rel_bias,
              norm_mem, w_mem_kv, w_out, norm_ff2, w_ff2_in, w_ff2_out):
    b_p, t_p, _ = x_prompt.shape
    t_s = x_sample.shape[1]
    pos_p = jnp.arange(t_p, dtype=jnp.int32)
    pos_s = PAST_LEN + jnp.arange(t_s, dtype=jnp.int32)
    keep_a = min(WINDOW, t_p)
    keep_b = min(BAND_ROWS, t_p)
    empty_a = jnp.zeros((b_p, 0, H_A_KV, HEAD_DIM), x_prompt.dtype)
    empty_b = jnp.zeros((b_p, 0, H_B, HEAD_DIM), x_prompt.dtype)
    xp, xs = x_prompt, x_sample
    ak_p, av_p, bk_p, bv_p, mk_p, mv_p = [], [], [], [], [], []
    ak_s, av_s, bk_s, bv_s = [], [], [], []
    for l in range(DEPTH):
        lw = (norm_ff1[l], w_ff1_in[l], w_ff1_out[l], norm_mix[l], w_in[l], qk_gain[l], attn_sinks[l],
              rel_bias[l], w_out[l], norm_ff2[l], w_ff2_in[l], w_ff2_out[l])
        mk, mv = memory_kv(mem_prompt, norm_mem[l], w_mem_kv[l], qk_gain[l, 5])
        xp, ka, va, kb, vb = trunk_layer(xp, pos_p, empty_a, empty_a, empty_b, empty_b, mk, mv, lw)
        ak_p.append(ka[:, t_p - keep_a:])
        av_p.append(va[:, t_p - keep_a:])
        bk_p.append(kb[:, t_p - keep_b:])
        bv_p.append(vb[:, t_p - keep_b:])
        mk_p.append(mk)
        mv_p.append(mv)
        xs, ka, va, kb, vb = trunk_layer(xs, pos_s, cache_a_k[l], cache_a_v[l], cache_b_k[l], cache_b_v[l],
                                         cache_mem_k[l], cache_mem_v[l], lw)
        ak_s.append(ka)
        av_s.append(va)
        bk_s.append(kb)
        bv_s.append(vb)
    return (xp, xs,
            jnp.stack(ak_p), jnp.stack(av_p), jnp.stack(bk_p), jnp.stack(bv_p),
            jnp.stack(mk_p), jnp.stack(mv_p),
            jnp.stack(ak_s), jnp.stack(av_s), jnp.stack(bk_s), jnp.stack(bv_s))
```

```python
import functools

import numpy as np
import jax
import jax.numpy as jnp
from jax import lax
from jax.experimental import pallas as pl
from jax.experimental.pallas import tpu as pltpu

CHUNK = 64
HEAD_DIM = 128
H_A = 8
H_A_KV = 2
WIN_CHUNKS = 2
WINDOW = WIN_CHUNKS * CHUNK
H_B = 4
BAND_CHUNKS = 8
BAND_ROWS = BAND_CHUNKS * CHUNK
REL_CLIP = 128
H_M = 4
ROT_DIM = HEAD_DIM // 4
ROPE_THETA = 500000.0
EPS = 1e-6
PAST_LEN = 1024
SCALE = HEAD_DIM ** -0.5

Q_COLS = (H_A + H_B + H_M) * HEAD_DIM
KV_COLS = (2 * H_A_KV + 2 * H_B) * HEAD_DIM
QB_OFF = H_A * HEAD_DIM
QM_OFF = (H_A + H_B) * HEAD_DIM
KA_OFF = 0
VA_OFF = H_A_KV * HEAD_DIM
KB_OFF = 2 * H_A_KV * HEAD_DIM
VB_OFF = KB_OFF + H_B * HEAD_DIM

V7X_VMEM_LIMIT_BYTES = 56 * 1024 * 1024
HEAD_GROUP = 4
FF_TILE = 512
MASK_VALUE = -1e30

_NT_DIMS = (((1,), (1,)), ((), ()))


def _row_tile(rows, preferred):
    tile = min(rows, preferred)
    while rows % tile:
        tile //= 2
    return tile


def _params(semantics):
    return pltpu.CompilerParams(dimension_semantics=semantics,
                                vmem_limit_bytes=V7X_VMEM_LIMIT_BYTES)


def _rms_rows(x, g):
    ms = jnp.mean(x * x, axis=-1, keepdims=True)
    return x * lax.rsqrt(ms + EPS) * g


def _ffn_kernel(x_ref, g_ref, wi_ref, wo_ref, o_ref, xn_ref):
    j = pl.program_id(1)

    @pl.when(j == 0)
    def _():
        x = x_ref[...]
        xn_ref[...] = _rms_rows(x, g_ref[...]).astype(xn_ref.dtype)
        o_ref[...] = x

    h = jnp.dot(xn_ref[...], wi_ref[...], preferred_element_type=jnp.float32)
    gate = h[:, :FF_TILE]
    up = h[:, FF_TILE:]
    act = (gate * jax.nn.sigmoid(gate) * up * 0.5).astype(jnp.bfloat16)
    o_ref[...] += jnp.dot(act, wo_ref[...], preferred_element_type=jnp.float32)


def _ffn(x, g, wi, wo, row_tile):
    rows, d = x.shape
    n_ff = wo.shape[0] // FF_TILE
    tm = _row_tile(rows, row_tile)
    return pl.pallas_call(
        _ffn_kernel,
        grid=(rows // tm, n_ff),
        in_specs=[
            pl.BlockSpec((tm, d), lambda i, j: (i, 0)),
            pl.BlockSpec((1, d), lambda i, j: (0, 0)),
            pl.BlockSpec((d, 2 * FF_TILE), lambda i, j: (0, j)),
            pl.BlockSpec((FF_TILE, d), lambda i, j: (j, 0)),
        ],
        out_specs=pl.BlockSpec((tm, d), lambda i, j: (i, 0)),
        out_shape=jax.ShapeDtypeStruct((rows, d), jnp.float32),
        scratch_shapes=[pltpu.VMEM((tm, d), jnp.bfloat16)],
        compiler_params=_params(("parallel", "arbitrary")),
        name="ffn",
    )(x, g, wi, wo)


def _prep_ffn_weights(w_i, w_o):
    depth, d, two_f = w_i.shape
    f = two_f // 2
    f_pad = -(-f // FF_TILE) * FF_TILE
    n_ff = f_pad // FF_TILE
    wi = w_i.astype(jnp.bfloat16).reshape(depth, d, 2, f)
    wi = jnp.pad(wi, ((0, 0), (0, 0), (0, 0), (0, f_pad - f)))
    wi = wi.reshape(depth, d, 2, n_ff, FF_TILE).transpose(0, 1, 3, 2, 4)
    wi = wi.reshape(depth, d, 2 * f_pad)
    wo = jnp.pad(w_o.astype(jnp.bfloat16), ((0, 0), (0, f_pad - f), (0, 0)))
    return wi, wo


def _head_plan():
    plan = []
    for h in range(H_A):
        plan.append((0, True, True, True, h * HEAD_DIM))
    for h in range(H_A_KV):
        plan.append((1, True, False, False, KA_OFF + h * HEAD_DIM))
    for h in range(H_A_KV):
        plan.append((None, False, False, False, VA_OFF + h * HEAD_DIM))
    for h in range(H_B):
        plan.append((2, False, True, True, QB_OFF + h * HEAD_DIM))
    for h in range(H_B):
        plan.append((3, False, False, False, KB_OFF + h * HEAD_DIM))
    for h in range(H_B):
        plan.append((None, False, False, False, VB_OFF + h * HEAD_DIM))
    for h in range(H_M):
        plan.append((4, False, True, True, QM_OFF + h * HEAD_DIM))
    return plan


_PLAN = _head_plan()
IN_COLS = len(_PLAN) * HEAD_DIM


def _proj_kernel(x_ref, g_ref, w_ref, qkg_ref, cos_ref, sa_ref, sb_ref, q_ref, kv_ref):
    xn = _rms_rows(x_ref[...], g_ref[...]).astype(jnp.bfloat16)
    cos = cos_ref[...]
    sa = sa_ref[...]
    sb = sb_ref[...]
    gw = HEAD_GROUP * HEAD_DIM
    for grp in range(len(_PLAN) // HEAD_GROUP):
        p = jnp.dot(xn, w_ref[:, grp * gw:(grp + 1) * gw], preferred_element_type=jnp.float32)
        for hh in range(HEAD_GROUP):
            gain, rotary, scaled, is_query, col = _PLAN[grp * HEAD_GROUP + hh]
            z = p[:, hh * HEAD_DIM:(hh + 1) * HEAD_DIM]
            if gain is not None:
                z = _rms_rows(z, qkg_ref[gain:gain + 1, :])
            if rotary:
                z = (z * cos + pltpu.roll(z, HEAD_DIM - ROT_DIM // 2, 1) * sa
                     + pltpu.roll(z, ROT_DIM // 2, 1) * sb)
            if scaled:
                z = z * SCALE
            dst = q_ref if is_query else kv_ref
            dst[:, col:col + HEAD_DIM] = z.astype(dst.dtype)


def _rope_tables(pos):
    half = ROT_DIM // 2
    inv_freq = ROPE_THETA ** (-jnp.arange(half, dtype=jnp.float32) / half)
    ang = pos.astype(jnp.float32)[:, None] * inv_freq[None, :]
    cos = jnp.cos(ang)
    sin = jnp.sin(ang)
    t = pos.shape[0]
    ones = jnp.ones((t, HEAD_DIM - ROT_DIM), jnp.float32)
    cos_t = jnp.concatenate([cos, cos, ones], axis=1)
    sa_t = jnp.concatenate([-sin, jnp.zeros((t, HEAD_DIM - half), jnp.float32)], axis=1)
    sb_t = jnp.concatenate([jnp.zeros((t, half), jnp.float32), sin,
                            jnp.zeros((t, HEAD_DIM - ROT_DIM), jnp.float32)], axis=1)
    return cos_t, sa_t, sb_t


def _proj(x, g, w_in, qk_gain, tables, seq_len, row_tile):
    rows, d = x.shape
    tm = _row_tile(rows, row_tile)
    if tm <= seq_len:
        assert seq_len % tm == 0
        per_seq = seq_len // tm
        tab_map = lambda i: (i % per_seq, 0)
    else:
        assert tm % seq_len == 0
        tables = tuple(jnp.tile(t, (tm // seq_len, 1)) for t in tables)
        tab_map = lambda i: (0, 0)
    tab_spec = pl.BlockSpec((tm, HEAD_DIM), tab_map)
    return pl.pallas_call(
        _proj_kernel,
        grid=(rows // tm,),
        in_specs=[
            pl.BlockSpec((tm, d), lambda i: (i, 0)),
            pl.BlockSpec((1, d), lambda i: (0, 0)),
            pl.BlockSpec((d, IN_COLS), lambda i: (0, 0)),
            pl.BlockSpec(qk_gain.shape, lambda i: (0, 0)),
            tab_spec, tab_spec, tab_spec,
        ],
        out_specs=[
            pl.BlockSpec((tm, Q_COLS), lambda i: (i, 0)),
            pl.BlockSpec((tm, KV_COLS), lambda i: (i, 0)),
        ],
        out_shape=[
            jax.ShapeDtypeStruct((rows, Q_COLS), jnp.bfloat16),
            jax.ShapeDtypeStruct((rows, KV_COLS), jnp.float32),
        ],
        compiler_params=_params(("parallel",)),
        name="proj",
    )(x, g, w_in, qk_gain, *tables)


def _memkv_kernel(x_ref, g_ref, w_ref, gk_ref, o_ref):
    xn = _rms_rows(x_ref[...], g_ref[...]).astype(jnp.bfloat16)
    gw = H_M * HEAD_DIM
    for part in range(2):
        p = jnp.dot(xn, w_ref[:, part * gw:(part + 1) * gw], preferred_element_type=jnp.float32)
        for h in range(H_M):
            z = p[:, h * HEAD_DIM:(h + 1) * HEAD_DIM]
            if part == 0:
                z = _rms_rows(z, gk_ref[...])
            col = part * gw + h * HEAD_DIM
            o_ref[:, col:col + HEAD_DIM] = z


def _memkv(mem, g, w_kv, g_k, row_tile):
    rows, d = mem.shape
    tm = _row_tile(rows, row_tile)
    cols = 2 * H_M * HEAD_DIM
    return pl.pallas_call(
        _memkv_kernel,
        grid=(rows // tm,),
        in_specs=[
            pl.BlockSpec((tm, d), lambda i: (i, 0)),
            pl.BlockSpec((1, d), lambda i: (0, 0)),
            pl.BlockSpec((d, cols), lambda i: (0, 0)),
            pl.BlockSpec((1, HEAD_DIM), lambda i: (0, 0)),
        ],
        out_specs=pl.BlockSpec((tm, cols), lambda i: (i, 0)),
        out_shape=jax.ShapeDtypeStruct((rows, cols), jnp.float32),
        compiler_params=_params(("parallel",)),
        name="memkv",
    )(mem, g, w_kv, g_k)


def _softmax_pv(s, v, sink=None):
    m = jnp.max(s, axis=-1, keepdims=True)
    if sink is not None:
        m = jnp.maximum(m, sink)
    e = jnp.exp(s - m)
    den = jnp.sum(e, axis=-1, keepdims=True)
    if sink is not None:
        den = den + jnp.exp(sink - m)
    return jnp.dot(e.astype(v.dtype), v, preferred_element_type=jnp.float32) / den


def _attn_kernel(sink_ref, q_ref, ka_ref, va_ref, kb_ref, vb_ref, mk_ref, mv_ref,
                 ba_ref, bb_ref, o_ref, *, q_tile, win_a, win_b):
    row0 = pl.multiple_of(pl.program_id(1) * q_tile, q_tile)

    def scores(q, k):
        return lax.dot_general(q, k, _NT_DIMS, preferred_element_type=jnp.float32)

    bias_a = ba_ref[0]
    for h in range(H_A):
        kv = slice((h // (H_A // H_A_KV)) * HEAD_DIM, (h // (H_A // H_A_KV) + 1) * HEAD_DIM)
        cols = slice(h * HEAD_DIM, (h + 1) * HEAD_DIM)
        s = scores(q_ref[0, :, cols], ka_ref[0, pl.ds(row0, win_a), kv]) + bias_a
        o = _softmax_pv(s, va_ref[0, pl.ds(row0, win_a), kv], sink_ref[h])
        o_ref[0, :, cols] = o.astype(o_ref.dtype)

    for h in range(H_B):
        kv = slice(h * HEAD_DIM, (h + 1) * HEAD_DIM)
        cols = slice(QB_OFF + h * HEAD_DIM, QB_OFF + (h + 1) * HEAD_DIM)
        s = scores(q_ref[0, :, cols], kb_ref[0, pl.ds(row0, win_b), kv]) + bb_ref[0, h]
        o = _softmax_pv(s, vb_ref[0, pl.ds(row0, win_b), kv])
        o_ref[0, :, cols] = o.astype(o_ref.dtype)

    for h in range(H_M):
        kv = slice(h * HEAD_DIM, (h + 1) * HEAD_DIM)
        cols = slice(QM_OFF + h * HEAD_DIM, QM_OFF + (h + 1) * HEAD_DIM)
        s = scores(q_ref[0, :, cols], mk_ref[0, :, kv])
        o = _softmax_pv(s, mv_ref[0, :, kv])
        o_ref[0, :, cols] = o.astype(o_ref.dtype)


def _band_masks(q_tile, past_rows, n_past_chunks, n_tiles, lo, hi):
    win = past_rows + q_tile
    r = np.arange(q_tile)[:, None] // CHUNK
    w = np.arange(win)[None, :]
    band = (w // CHUNK >= r) & (w // CHUNK <= r + n_past_chunks)
    masks = []
    for i in range(n_tiles):
        pidx = i * q_tile + w
        masks.append(band & (pidx >= lo) & (pidx < hi))
    n_var = n_tiles
    while n_var > 1 and np.array_equal(masks[n_var - 2], masks[n_tiles - 1]):
        n_var -= 1
    for i in range(n_var - 1, n_tiles):
        assert np.array_equal(masks[i], masks[n_var - 1])
    return np.stack(masks[:n_var])


def _attention_call(q, ka, va, kb, vb, mk, mv, sinks, bias_a, bias_b, q_tile):
    b, tq, _ = q.shape
    assert ka.shape[1] == WINDOW + tq and kb.shape[1] == BAND_ROWS + tq
    n_tiles = tq // q_tile
    win_a = WINDOW + q_tile
    win_b = BAND_ROWS + q_tile
    na = bias_a.shape[0]
    nb = bias_b.shape[0]
    n_mem = mk.shape[1]

    def whole(arr):
        return pl.BlockSpec((1,) + arr.shape[1:], lambda bi, i: (bi, 0, 0))

    kern = functools.partial(_attn_kernel, q_tile=q_tile, win_a=win_a, win_b=win_b)
    return pl.pallas_call(
        kern,
        grid=(b, n_tiles),
        in_specs=[
            pl.BlockSpec(memory_space=pltpu.SMEM),
            pl.BlockSpec((1, q_tile, Q_COLS), lambda bi, i: (bi, i, 0)),
            whole(ka), whole(va), whole(kb), whole(vb), whole(mk), whole(mv),
            pl.BlockSpec((1, q_tile, win_a), lambda bi, i: (jnp.minimum(i, na - 1), 0, 0)),
            pl.BlockSpec((1, H_B, q_tile, win_b), lambda bi, i: (jnp.minimum(i, nb - 1), 0, 0, 0)),
        ],
        out_specs=pl.BlockSpec((1, q_tile, Q_COLS), lambda bi, i: (bi, i, 0)),
        out_shape=jax.ShapeDtypeStruct((b, tq, Q_COLS), jnp.bfloat16),
        compiler_params=_params(("parallel", "arbitrary")),
        name="attn",
    )(sinks, q, ka, va, kb, vb, mk, mv, bias_a, bias_b)


def _attn_biases(rel_tab, q_tile, n_tiles, past_a, past_b, new_rows):
    mask_a = _band_masks(q_tile, WINDOW, WIN_CHUNKS, n_tiles, WINDOW - past_a, WINDOW + new_rows)
    mask_b = _band_masks(q_tile, BAND_ROWS, BAND_CHUNKS, n_tiles, BAND_ROWS - past_b,
                         BAND_ROWS + new_rows)
    bias_a = jnp.asarray(np.where(mask_a, 0.0, MASK_VALUE), jnp.float32)
    r = np.arange(q_tile)[:, None]
    w = np.arange(BAND_ROWS + q_tile)[None, :]
    idx = np.clip(BAND_ROWS + r - w, -REL_CLIP, REL_CLIP) + REL_CLIP
    rel = rel_tab[:, idx].astype(jnp.float32)
    bias_b = jnp.where(jnp.asarray(mask_b)[:, None], rel[None], MASK_VALUE)
    return bias_a, bias_b


def _outproj_kernel(x_ref, o_ref, w_ref, y_ref):
    y_ref[...] = x_ref[...] + jnp.dot(o_ref[...], w_ref[...], preferred_element_type=jnp.float32)


def _outproj(x, o, w, row_tile):
    rows, d = x.shape
    tm = _row_tile(rows, row_tile)
    return pl.pallas_call(
        _outproj_kernel,
        grid=(rows // tm,),
        in_specs=[
            pl.BlockSpec((tm, d), lambda i: (i, 0)),
            pl.BlockSpec((tm, o.shape[1]), lambda i: (i, 0)),
            pl.BlockSpec(w.shape, lambda i: (0, 0)),
        ],
        out_specs=pl.BlockSpec((tm, d), lambda i: (i, 0)),
        out_shape=jax.ShapeDtypeStruct((rows, d), jnp.float32),
        compiler_params=_params(("parallel",)),
        name="outproj",
    )(x, o, w)


def _front_pad(new, past, past_rows, total_rows):
    b, t, c = new.shape
    parts = []
    lp = 0 if past is None else past.shape[1]
    if past_rows - lp:
        parts.append(jnp.zeros((b, past_rows - lp, c), jnp.bfloat16))
    if past is not None:
        parts.append(past.astype(jnp.bfloat16))
    parts.append(new.astype(jnp.bfloat16))
    if total_rows - past_rows - t:
        parts.append(jnp.zeros((b, total_rows - past_rows - t, c), jnp.bfloat16))
    return jnp.concatenate(parts, axis=1)


def _layer(x, batch, seq, tables, past, mem_k, mem_v, lw, row_tile, q_tile):
    (g_ff1, w1i, w1o, g_mix, w_in, qk_g, sinks, rel_tab, w_out, g_ff2, w2i, w2o) = lw
    x = _ffn(x, g_ff1, w1i, w1o, row_tile)
    q, kv = _proj(x, g_mix, w_in, qk_g, tables, seq, row_tile)

    tq = -(-seq // q_tile) * q_tile
    q3 = q.reshape(batch, seq, Q_COLS)
    if tq != seq:
        q3 = jnp.pad(q3, ((0, 0), (0, tq - seq), (0, 0)))
    kv3 = kv.reshape(batch, seq, KV_COLS)
    ka_new = kv3[:, :, KA_OFF:VA_OFF]
    va_new = kv3[:, :, VA_OFF:KB_OFF]
    kb_new = kv3[:, :, KB_OFF:VB_OFF]
    vb_new = kv3[:, :, VB_OFF:]
    pa_k, pa_v, pb_k, pb_v = past if past is not None else (None,) * 4
    ka = _front_pad(ka_new, pa_k, WINDOW, WINDOW + tq)
    va = _front_pad(va_new, pa_v, WINDOW, WINDOW + tq)
    kb = _front_pad(kb_new, pb_k, BAND_ROWS, BAND_ROWS + tq)
    vb = _front_pad(vb_new, pb_v, BAND_ROWS, BAND_ROWS + tq)
    past_a = 0 if pa_k is None else pa_k.shape[1]
    past_b = 0 if pb_k is None else pb_k.shape[1]
    bias_a, bias_b = _attn_biases(rel_tab, q_tile, tq // q_tile, past_a, past_b, seq)
    o = _attention_call(q3, ka, va, kb, vb, mem_k, mem_v, sinks, bias_a, bias_b, q_tile)
    o = o[:, :seq].reshape(batch * seq, Q_COLS)

    x = _outproj(x, o, w_out, row_tile)
    x = _ffn(x, g_ff2, w2i, w2o, row_tile)
    return x, ka_new, va_new, kb_new, vb_new


PROMPT_ROW_TILE = 512
PROMPT_Q_TILE = 256


def kernel(x_prompt, x_sample, cache_a_k, cache_a_v, cache_b_k, cache_b_v, cache_mem_k, cache_mem_v,
           mem_prompt, norm_ff1, w_ff1_in, w_ff1_out, norm_mix, w_in, qk_gain, attn_sinks, rel_bias,
           norm_mem, w_mem_kv, w_out, norm_ff2, w_ff2_in, w_ff2_out):
    b_p, t_p, d = x_prompt.shape
    b_s, t_s, _ = x_sample.shape
    depth = w_in.shape[0]
    n_mem = mem_prompt.shape[1]
    bf16 = jnp.bfloat16

    w1i, w1o = _prep_ffn_weights(w_ff1_in, w_ff1_out)
    w2i, w2o = _prep_ffn_weights(w_ff2_in, w_ff2_out)
    w_in_b = w_in.astype(bf16)
    w_out_b = w_out.astype(bf16)
    w_mem_b = w_mem_kv.astype(bf16)

    tab_p = _rope_tables(jnp.arange(t_p, dtype=jnp.int32))
    tab_s = _rope_tables(PAST_LEN + jnp.arange(t_s, dtype=jnp.int32))
    keep_a = min(WINDOW, t_p)
    keep_b = min(BAND_ROWS, t_p)

    def flat_heads(c):
        return c.reshape(c.shape[0], c.shape[1], -1)

    xp = x_prompt.reshape(b_p * t_p, d)
    xs = x_sample.reshape(b_s * t_s, d)
    mem_flat = mem_prompt.reshape(b_p * n_mem, d)
    outs = [[] for _ in range(10)]
    for l in range(depth):
        lw = (norm_ff1[l][None], w1i[l], w1o[l], norm_mix[l][None], w_in_b[l], qk_gain[l],
              attn_sinks[l], rel_bias[l], w_out_b[l], norm_ff2[l][None], w2i[l], w2o[l])
        mkv = _memkv(mem_flat, norm_mem[l][None], w_mem_b[l], qk_gain[l, 5][None], PROMPT_ROW_TILE)
        mkv = mkv.reshape(b_p, n_mem, 2 * H_M * HEAD_DIM)
        mk = mkv[:, :, :H_M * HEAD_DIM]
        mv = mkv[:, :, H_M * HEAD_DIM:]
        xp, ka, va, kb, vb = _layer(xp, b_p, t_p, tab_p, None, mk.astype(bf16), mv.astype(bf16), lw,
                                    PROMPT_ROW_TILE, PROMPT_Q_TILE)
        outs[0].append(ka[:, t_p - keep_a:].reshape(b_p, keep_a, H_A_KV, HEAD_DIM))
        outs[1].append(va[:, t_p - keep_a:].reshape(b_p, keep_a, H_A_KV, HEAD_DIM))
        outs[2].append(kb[:, t_p - keep_b:].reshape(b_p, keep_b, H_B, HEAD_DIM))
        outs[3].append(vb[:, t_p - keep_b:].reshape(b_p, keep_b, H_B, HEAD_DIM))
        outs[4].append(mk.reshape(b_p, n_mem, H_M, HEAD_DIM))
        outs[5].append(mv.reshape(b_p, n_mem, H_M, HEAD_DIM))

        past = (flat_heads(cache_a_k[l]), flat_heads(cache_a_v[l]),
                flat_heads(cache_b_k[l]), flat_heads(cache_b_v[l]))
        xs, ka, va, kb, vb = _layer(xs, b_s, t_s, tab_s, past,
                                    flat_heads(cache_mem_k[l]).astype(bf16),
                                    flat_heads(cache_mem_v[l]).astype(bf16), lw,
                                    PROMPT_ROW_TILE, CHUNK)
        outs[6].append(ka.reshape(b_s, t_s, H_A_KV, HEAD_DIM))
        outs[7].append(va.reshape(b_s, t_s, H_A_KV, HEAD_DIM))
        outs[8].append(kb.reshape(b_s, t_s, H_B, HEAD_DIM))
        outs[9].append(vb.reshape(b_s, t_s, H_B, HEAD_DIM))

    return (xp.reshape(b_p, t_p, d), xs.reshape(b_s, t_s, d)) + tuple(jnp.stack(o) for o in outs)
```

```python
import functools
import math

import numpy as np
import jax
import jax.numpy as jnp
from jax import lax
from jax.experimental import pallas as pl
from jax.experimental.pallas import tpu as pltpu

CHUNK = 64
HEAD_DIM = 128
H_A = 8
H_A_KV = 2
WIN_CHUNKS = 2
WINDOW = WIN_CHUNKS * CHUNK
H_B = 4
BAND_CHUNKS = 8
BAND_ROWS = BAND_CHUNKS * CHUNK
REL_CLIP = 128
H_M = 4
ROT_DIM = HEAD_DIM // 4
ROPE_THETA = 500000.0
EPS = 1e-6
PAST_LEN = 1024
SCALE = HEAD_DIM ** -0.5

Q_COLS = (H_A + H_B + H_M) * HEAD_DIM
KV_COLS = (2 * H_A_KV + 2 * H_B) * HEAD_DIM
MEM_COLS = 2 * H_M * HEAD_DIM
QB_OFF = H_A * HEAD_DIM
QM_OFF = (H_A + H_B) * HEAD_DIM
KA_OFF = 0
VA_OFF = H_A_KV * HEAD_DIM
KB_OFF = 2 * H_A_KV * HEAD_DIM
VB_OFF = KB_OFF + H_B * HEAD_DIM

V7X_VMEM_LIMIT_BYTES = 58 * 1024 * 1024
HEAD_GROUP = 4
FF_TILE = 512
FFN_ROW_TILE = 1024
ROW_TILE = 512
PROMPT_Q_TILE = 256
MASK_VALUE = -1e30

_NT_DIMS = (((1,), (1,)), ((), ()))


def _row_tile(rows, preferred):
    tile = min(rows, preferred)
    while rows % tile:
        tile //= 2
    return tile


def _params(semantics):
    return pltpu.CompilerParams(dimension_semantics=semantics,
                                vmem_limit_bytes=V7X_VMEM_LIMIT_BYTES)


def _rms_rows(x, g):
    ms = jnp.mean(x * x, axis=-1, keepdims=True)
    return x * lax.rsqrt(ms + EPS) * g


def _layer_spec(shape, index_map):
    return pl.BlockSpec((None,) + tuple(shape), index_map)


def _ffn_kernel(x_ref, g_ref, wg_ref, wu_ref, wo_ref, o_ref, xn_ref):
    @pl.when(pl.program_id(1) == 0)
    def _():
        x = x_ref[...]
        xn_ref[...] = _rms_rows(x, g_ref[...]).astype(xn_ref.dtype)
        o_ref[...] = x

    xn = xn_ref[...]
    gate = jnp.dot(xn, wg_ref[...], preferred_element_type=jnp.float32)
    up = jnp.dot(xn, wu_ref[...], preferred_element_type=jnp.float32)
    act = (gate * jax.nn.sigmoid(gate) * up * 0.5).astype(jnp.bfloat16)
    o_ref[...] += jnp.dot(act, wo_ref[...], preferred_element_type=jnp.float32)


def _ffn(x, g, wg, wu, wo, layer):
    rows, d = x.shape
    n_ff = wo.shape[1] // FF_TILE
    tm = _row_tile(rows, FFN_ROW_TILE)
    return pl.pallas_call(
        _ffn_kernel,
        grid=(rows // tm, n_ff),
        in_specs=[
            pl.BlockSpec((tm, d), lambda i, j: (i, 0)),
            _layer_spec((1, d), lambda i, j: (layer, 0, 0)),
            _layer_spec((d, FF_TILE), lambda i, j: (layer, 0, j)),
            _layer_spec((d, FF_TILE), lambda i, j: (layer, 0, j)),
            _layer_spec((FF_TILE, d), lambda i, j: (layer, j, 0)),
        ],
        out_specs=pl.BlockSpec((tm, d), lambda i, j: (i, 0)),
        out_shape=jax.ShapeDtypeStruct((rows, d), jnp.float32),
        scratch_shapes=[pltpu.VMEM((tm, d), jnp.bfloat16)],
        compiler_params=_params(("parallel", "arbitrary")),
        name="ffn",
    )(x, g, wg, wu, wo)


def _prep_ffn_weights(w_i, w_o):
    f = w_o.shape[1]
    pad = -f % FF_TILE
    wg = jnp.pad(w_i[:, :, :f].astype(jnp.bfloat16), ((0, 0), (0, 0), (0, pad)))
    wu = jnp.pad(w_i[:, :, f:].astype(jnp.bfloat16), ((0, 0), (0, 0), (0, pad)))
    wo = jnp.pad(w_o.astype(jnp.bfloat16), ((0, 0), (0, pad), (0, 0)))
    return wg, wu, wo


def _head_plan():
    plan = []
    for h in range(H_A):
        plan.append((0, True, True, h * HEAD_DIM))
    for h in range(H_A_KV):
        plan.append((1, True, False, KA_OFF + h * HEAD_DIM))
    for h in range(H_A_KV):
        plan.append((None, False, False, VA_OFF + h * HEAD_DIM))
    for h in range(H_B):
        plan.append((2, False, True, QB_OFF + h * HEAD_DIM))
    for h in range(H_B):
        plan.append((3, False, False, KB_OFF + h * HEAD_DIM))
    for h in range(H_B):
        plan.append((None, False, False, VB_OFF + h * HEAD_DIM))
    for h in range(H_M):
        plan.append((4, False, True, QM_OFF + h * HEAD_DIM))
    return plan


_PLAN = _head_plan()
IN_COLS = len(_PLAN) * HEAD_DIM


def _proj_kernel(x_ref, g_ref, w_ref, qkg_ref, cos_ref, sa_ref, sb_ref, q_ref, kvb_ref, kvf_ref):
    xn = _rms_rows(x_ref[...], g_ref[...]).astype(jnp.bfloat16)
    cos = cos_ref[...]
    sa = sa_ref[...]
    sb = sb_ref[...]
    gw = HEAD_GROUP * HEAD_DIM
    for grp in range(len(_PLAN) // HEAD_GROUP):
        p = jnp.dot(xn, w_ref[:, grp * gw:(grp + 1) * gw], preferred_element_type=jnp.float32)
        for hh in range(HEAD_GROUP):
            gain, rotary, is_query, col = _PLAN[grp * HEAD_GROUP + hh]
            z = p[:, hh * HEAD_DIM:(hh + 1) * HEAD_DIM]
            if gain is not None:
                z = _rms_rows(z, qkg_ref[gain:gain + 1, :])
            if rotary:
                z = (z * cos + pltpu.roll(z, HEAD_DIM - ROT_DIM // 2, 1) * sa
                     + pltpu.roll(z, ROT_DIM // 2, 1) * sb)
            cols = slice(col, col + HEAD_DIM)
            if is_query:
                q_ref[:, cols] = (z * SCALE).astype(q_ref.dtype)
            else:
                kvf_ref[:, cols] = z
                kvb_ref[:, cols] = z.astype(kvb_ref.dtype)


def _rope_tables(pos):
    half = ROT_DIM // 2
    inv_freq = ROPE_THETA ** (-jnp.arange(half, dtype=jnp.float32) / half)
    ang = pos.astype(jnp.float32)[:, None] * inv_freq[None, :]
    cos = jnp.cos(ang)
    sin = jnp.sin(ang)
    t = pos.shape[0]
    ones = jnp.ones((t, HEAD_DIM - ROT_DIM), jnp.float32)
    cos_t = jnp.concatenate([cos, cos, ones], axis=1)
    sa_t = jnp.concatenate([-sin, jnp.zeros((t, HEAD_DIM - half), jnp.float32)], axis=1)
    sb_t = jnp.concatenate([jnp.zeros((t, half), jnp.float32), sin,
                            jnp.zeros((t, HEAD_DIM - ROT_DIM), jnp.float32)], axis=1)
    return cos_t, sa_t, sb_t


def _proj(x, g, w_in, qk_gain, tables, seq_len, layer):
    rows, d = x.shape
    tm = _row_tile(rows, ROW_TILE)
    if tm <= seq_len:
        assert seq_len % tm == 0
        per_seq = seq_len // tm
        tab_map = lambda i: (i % per_seq, 0)
    else:
        assert tm % seq_len == 0
        tables = tuple(jnp.tile(t, (tm // seq_len, 1)) for t in tables)
        tab_map = lambda i: (0, 0)
    tab_spec = pl.BlockSpec((tm, HEAD_DIM), tab_map)
    return pl.pallas_call(
        _proj_kernel,
        grid=(rows // tm,),
        in_specs=[
            pl.BlockSpec((tm, d), lambda i: (i, 0)),
            _layer_spec((1, d), lambda i: (layer, 0, 0)),
            _layer_spec((d, IN_COLS), lambda i: (layer, 0, 0)),
            _layer_spec(qk_gain.shape[1:], lambda i: (layer, 0, 0)),
            tab_spec, tab_spec, tab_spec,
        ],
        out_specs=[
            pl.BlockSpec((tm, Q_COLS), lambda i: (i, 0)),
            pl.BlockSpec((tm, KV_COLS), lambda i: (i, 0)),
            pl.BlockSpec((tm, KV_COLS), lambda i: (i, 0)),
        ],
        out_shape=[
            jax.ShapeDtypeStruct((rows, Q_COLS), jnp.bfloat16),
            jax.ShapeDtypeStruct((rows, KV_COLS), jnp.bfloat16),
            jax.ShapeDtypeStruct((rows, KV_COLS), jnp.float32),
        ],
        compiler_params=_params(("parallel",)),
        name="proj",
    )(x, g, w_in, qk_gain, *tables)


def _memkv_kernel(x_ref, g_ref, w_ref, qkg_ref, of_ref, ob_ref):
    xn = _rms_rows(x_ref[...], g_ref[...]).astype(jnp.bfloat16)
    gw = H_M * HEAD_DIM
    for part in range(2):
        p = jnp.dot(xn, w_ref[:, part * gw:(part + 1) * gw], preferred_element_type=jnp.float32)
        for h in range(H_M):
            z = p[:, h * HEAD_DIM:(h + 1) * HEAD_DIM]
            if part == 0:
                z = _rms_rows(z, qkg_ref[5:6, :])
            cols = slice(part * gw + h * HEAD_DIM, part * gw + (h + 1) * HEAD_DIM)
            of_ref[:, cols] = z
            ob_ref[:, cols] = z.astype(ob_ref.dtype)


def _memkv(mem, g, w_kv, qk_gain, layer):
    rows, d = mem.shape
    tm = _row_tile(rows, ROW_TILE)
    return pl.pallas_call(
        _memkv_kernel,
        grid=(rows // tm,),
        in_specs=[
            pl.BlockSpec((tm, d), lambda i: (i, 0)),
            _layer_spec((1, d), lambda i: (layer, 0, 0)),
            _layer_spec((d, MEM_COLS), lambda i: (layer, 0, 0)),
            _layer_spec(qk_gain.shape[1:], lambda i: (layer, 0, 0)),
        ],
        out_specs=[pl.BlockSpec((tm, MEM_COLS), lambda i: (i, 0))] * 2,
        out_shape=[jax.ShapeDtypeStruct((rows, MEM_COLS), jnp.float32),
                   jax.ShapeDtypeStruct((rows, MEM_COLS), jnp.bfloat16)],
        compiler_params=_params(("parallel",)),
        name="memkv",
    )(mem, g, w_kv, qk_gain)


def _attn_kernel(sink_ref, q_ref, kv_ref, mkv_ref, ba_ref, bb_ref, o_ref, *,
                 layer, q_tile, base, pieces_a, pieces_b):
    row0 = pl.multiple_of(pl.program_id(1) * q_tile, q_tile)

    def piece_rows(off, size):
        start = base + row0 + off
        if base + off < 0:
            start = jnp.maximum(start, 0)
        return pl.ds(pl.multiple_of(start, math.gcd(q_tile, abs(off), base)), size)

    def attend(q_col, rows_list, k_ref, k_col, v_col, bias, sink=None):
        q_cols = slice(q_col, q_col + HEAD_DIM)
        k_cols = slice(k_col, k_col + HEAD_DIM)
        v_cols = slice(v_col, v_col + HEAD_DIM)
        q = q_ref[0, :, q_cols]
        parts = [lax.dot_general(q, k_ref[0, rows, k_cols], _NT_DIMS,
                                 preferred_element_type=jnp.float32) for rows in rows_list]
        s = parts[0] if len(parts) == 1 else jnp.concatenate(parts, axis=1)
        if bias is not None:
            s = s + bias
        m = jnp.max(s, axis=-1, keepdims=True)
        if sink is not None:
            m = jnp.maximum(m, sink)
        e = jnp.exp(s - m)
        den = jnp.sum(e, axis=-1, keepdims=True)
        if sink is not None:
            den = den + jnp.exp(sink - m)
        e = e.astype(jnp.bfloat16)
        acc = None
        lo = 0
        for rows in rows_list:
            size = rows.size
            pv = jnp.dot(e[:, lo:lo + size], k_ref[0, rows, v_cols], preferred_element_type=jnp.float32)
            acc = pv if acc is None else acc + pv
            lo += size
        o_ref[0, :, q_cols] = (acc / den).astype(o_ref.dtype)

    rows_a = [piece_rows(off, size) for off, size in pieces_a]
    rows_b = [piece_rows(off, size) for off, size in pieces_b]
    rows_m = [pl.ds(0, mkv_ref.shape[1])]
    group = H_A // H_A_KV
    bias_a = ba_ref[0]
    for h in range(H_A):
        attend(h * HEAD_DIM, rows_a, kv_ref, KA_OFF + (h // group) * HEAD_DIM,
               VA_OFF + (h // group) * HEAD_DIM, bias_a, sink_ref[layer, h])
    for h in range(H_B):
        attend(QB_OFF + h * HEAD_DIM, rows_b, kv_ref, KB_OFF + h * HEAD_DIM,
               VB_OFF + h * HEAD_DIM, bb_ref[0, h])
    for h in range(H_M):
        attend(QM_OFF + h * HEAD_DIM, rows_m, mkv_ref, h * HEAD_DIM,
               (H_M + h) * HEAD_DIM, None)


def _window_pieces(past_rows, q_tile, base):
    size = past_rows if base >= past_rows else math.gcd(past_rows, q_tile)
    pieces = [(-past_rows + k * size, size) for k in range(past_rows // size)]
    return pieces + [(0, q_tile)]


def _band_masks(q_tile, past_rows, n_past_chunks, n_tiles, first_real, new_rows):
    win = past_rows + q_tile
    r = np.arange(q_tile)[:, None] // CHUNK
    w = np.arange(win)[None, :]
    band = (w // CHUNK >= r) & (w // CHUNK <= r + n_past_chunks)
    masks = []
    for i in range(n_tiles):
        row = i * q_tile + w - past_rows
        masks.append(band & (row >= first_real) & (row < new_rows))
    n_var = n_tiles
    while n_var > 1 and np.array_equal(masks[n_var - 2], masks[n_tiles - 1]):
        n_var -= 1
    return np.stack(masks[:n_var])


def _rel_bias_tiles(rel_tab, q_tile):
    win = BAND_ROWS + q_tile
    n = win + q_tile
    k = np.arange(n)
    dist = np.where(k < win, BAND_ROWS - k, BAND_ROWS + n - k)
    seq = rel_tab[:, :, np.clip(dist, -REL_CLIP, REL_CLIP) + REL_CLIP].astype(jnp.float32)
    flat = jnp.tile(seq, (1, 1, q_tile))[:, :, :q_tile * (n - 1)]
    return flat.reshape(rel_tab.shape[0], H_B, q_tile, n - 1)[..., :win]


def _attn_biases(rel_tab, q_tile, n_tiles, past_a, past_b, new_rows):
    mask_a = _band_masks(q_tile, WINDOW, WIN_CHUNKS, n_tiles, -past_a, new_rows)
    mask_b = _band_masks(q_tile, BAND_ROWS, BAND_CHUNKS, n_tiles, -past_b, new_rows)
    bias_a = jnp.asarray(np.where(mask_a, 0.0, MASK_VALUE), jnp.float32)
    rel = _rel_bias_tiles(rel_tab, q_tile)
    bias_b = jnp.where(jnp.asarray(mask_b)[None, :, None], rel[:, None], MASK_VALUE)
    return bias_a, bias_b


def _attention(q, kv, mkv, sinks, bias_a, bias_b, q_tile, base, layer):
    b, tq, _ = q.shape
    assert kv.shape[1] == base + tq
    n_a = bias_a.shape[0]
    n_b = bias_b.shape[1]
    win_a = WINDOW + q_tile
    win_b = BAND_ROWS + q_tile
    kern = functools.partial(_attn_kernel, layer=layer, q_tile=q_tile, base=base,
                             pieces_a=_window_pieces(WINDOW, q_tile, base),
                             pieces_b=_window_pieces(BAND_ROWS, q_tile, base))
    return pl.pallas_call(
        kern,
        grid=(b, tq // q_tile),
        in_specs=[
            pl.BlockSpec(memory_space=pltpu.SMEM),
            pl.BlockSpec((1, q_tile, Q_COLS), lambda bi, i: (bi, i, 0)),
            pl.BlockSpec((1,) + kv.shape[1:], lambda bi, i: (bi, 0, 0)),
            pl.BlockSpec((1,) + mkv.shape[1:], lambda bi, i: (bi, 0, 0)),
            pl.BlockSpec((1, q_tile, win_a), lambda bi, i: (jnp.minimum(i, n_a - 1), 0, 0)),
            _layer_spec((1, H_B, q_tile, win_b),
                        lambda bi, i: (layer, jnp.minimum(i, n_b - 1), 0, 0, 0)),
        ],
        out_specs=pl.BlockSpec((1, q_tile, Q_COLS), lambda bi, i: (bi, i, 0)),
        out_shape=jax.ShapeDtypeStruct((b, tq, Q_COLS), jnp.bfloat16),
        compiler_params=_params(("parallel", "arbitrary")),
        name="attn",
    )(sinks, q, kv, mkv, bias_a, bias_b)


def _outproj_kernel(x_ref, o_ref, w_ref, y_ref):
    y_ref[...] = x_ref[...] + jnp.dot(o_ref[...], w_ref[...], preferred_element_type=jnp.float32)


def _outproj(x, o, w, layer):
    rows, d = x.shape
    tm = _row_tile(rows, ROW_TILE)
    return pl.pallas_call(
        _outproj_kernel,
        grid=(rows // tm,),
        in_specs=[
            pl.BlockSpec((tm, d), lambda i: (i, 0)),
            pl.BlockSpec((tm, o.shape[1]), lambda i: (i, 0)),
            _layer_spec(w.shape[1:], lambda i: (layer, 0, 0)),
        ],
        out_specs=pl.BlockSpec((tm, d), lambda i: (i, 0)),
        out_shape=jax.ShapeDtypeStruct((rows, d), jnp.float32),
        compiler_params=_params(("parallel",)),
        name="outproj",
    )(x, o, w)


def kernel(x_prompt, x_sample, cache_a_k, cache_a_v, cache_b_k, cache_b_v, cache_mem_k, cache_mem_v,
           mem_prompt, norm_ff1, w_ff1_in, w_ff1_out, norm_mix, w_in, qk_gain, attn_sinks, rel_bias,
           norm_mem, w_mem_kv, w_out, norm_ff2, w_ff2_in, w_ff2_out):
    b_p, t_p, d = x_prompt.shape
    b_s, t_s, _ = x_sample.shape
    depth = w_in.shape[0]
    n_mem = mem_prompt.shape[1]
    bf16 = jnp.bfloat16
    assert t_p % PROMPT_Q_TILE == 0 and t_s <= CHUNK

    w1g, w1u, w1o = _prep_ffn_weights(w_ff1_in, w_ff1_out)
    w2g, w2u, w2o = _prep_ffn_weights(w_ff2_in, w_ff2_out)
    w_in_b = w_in.astype(bf16)
    w_out_b = w_out.astype(bf16)
    w_mem_b = w_mem_kv.astype(bf16)
    g_ff1 = norm_ff1[:, None, :]
    g_ff2 = norm_ff2[:, None, :]
    g_mix = norm_mix[:, None, :]
    g_mem = norm_mem[:, None, :]

    tab_p = _rope_tables(jnp.arange(t_p, dtype=jnp.int32))
    tab_s = _rope_tables(PAST_LEN + jnp.arange(t_s, dtype=jnp.int32))
    keep_a = min(WINDOW, t_p)
    keep_b = min(BAND_ROWS, t_p)
    la = cache_a_k.shape[2]
    lb = cache_b_k.shape[2]

    bias_a_p, bias_b_p = _attn_biases(rel_bias, PROMPT_Q_TILE, t_p // PROMPT_Q_TILE, 0, 0, t_p)
    bias_a_s, bias_b_s = _attn_biases(rel_bias, CHUNK, 1, la, lb, t_s)

    def heads_flat(c):
        return c.reshape(c.shape[:3] + (-1,)).astype(bf16)

    def front(c, rows):
        return jnp.pad(c, ((0, 0), (0, 0), (rows - c.shape[2], 0), (0, 0)))

    past_kv = jnp.concatenate(
        [front(heads_flat(cache_a_k), BAND_ROWS), front(heads_flat(cache_a_v), BAND_ROWS),
         front(heads_flat(cache_b_k), BAND_ROWS), front(heads_flat(cache_b_v), BAND_ROWS)], axis=3)
    mkv_s = jnp.concatenate([heads_flat(cache_mem_k), heads_flat(cache_mem_v)], axis=3)

    xp = x_prompt.reshape(b_p * t_p, d)
    xs = x_sample.reshape(b_s * t_s, d)
    mem_flat = mem_prompt.reshape(b_p * n_mem, d)
    outs = [[] for _ in range(10)]
    for l in range(depth):
        mkv_f, mkv_b = _memkv(mem_flat, g_mem, w_mem_b, qk_gain, l)
        xp = _ffn(xp, g_ff1, w1g, w1u, w1o, l)
        q, kvb, kvf = _proj(xp, g_mix, w_in_b, qk_gain, tab_p, t_p, l)
        o = _attention(q.reshape(b_p, t_p, Q_COLS), kvb.reshape(b_p, t_p, KV_COLS),
                       mkv_b.reshape(b_p, n_mem, MEM_COLS), attn_sinks, bias_a_p, bias_b_p,
                       PROMPT_Q_TILE, 0, l)
        xp = _outproj(xp, o.reshape(b_p * t_p, Q_COLS), w_out_b, l)
        xp = _ffn(xp, g_ff2, w2g, w2u, w2o, l)
        kvf = kvf.reshape(b_p, t_p, KV_COLS)
        mkv_f = mkv_f.reshape(b_p, n_mem, MEM_COLS)
        outs[0].append(kvf[:, t_p - keep_a:, KA_OFF:VA_OFF].reshape(b_p, keep_a, H_A_KV, HEAD_DIM))
        outs[1].append(kvf[:, t_p - keep_a:, VA_OFF:KB_OFF].reshape(b_p, keep_a, H_A_KV, HEAD_DIM))
        outs[2].append(kvf[:, t_p - keep_b:, KB_OFF:VB_OFF].reshape(b_p, keep_b, H_B, HEAD_DIM))
        outs[3].append(kvf[:, t_p - keep_b:, VB_OFF:].reshape(b_p, keep_b, H_B, HEAD_DIM))
        outs[4].append(mkv_f[:, :, :H_M * HEAD_DIM].reshape(b_p, n_mem, H_M, HEAD_DIM))
        outs[5].append(mkv_f[:, :, H_M * HEAD_DIM:].reshape(b_p, n_mem, H_M, HEAD_DIM))

        xs = _ffn(xs, g_ff1, w1g, w1u, w1o, l)
        q, kvb, kvf = _proj(xs, g_mix, w_in_b, qk_gain, tab_s, t_s, l)
        q = jnp.pad(q.reshape(b_s, t_s, Q_COLS), ((0, 0), (0, CHUNK - t_s), (0, 0)))
        kv = jnp.concatenate(
            [past_kv[l], kvb.reshape(b_s, t_s, KV_COLS), jnp.zeros((b_s, CHUNK - t_s, KV_COLS), bf16)],
            axis=1)
        o = _attention(q, kv, mkv_s[l], attn_sinks, bias_a_s, bias_b_s, CHUNK, BAND_ROWS, l)
        xs = _outproj(xs, o[:, :t_s].reshape(b_s * t_s, Q_COLS), w_out_b, l)
        xs = _ffn(xs, g_ff2, w2g, w2u, w2o, l)
        kvf = kvf.reshape(b_s, t_s, KV_COLS)
        outs[6].append(kvf[:, :, KA_OFF:VA_OFF].reshape(b_s, t_s, H_A_KV, HEAD_DIM))
        outs[7].append(kvf[:, :, VA_OFF:KB_OFF].reshape(b_s, t_s, H_A_KV, HEAD_DIM))
        outs[8].append(kvf[:, :, KB_OFF:VB_OFF].reshape(b_s, t_s, H_B, HEAD_DIM))
        outs[9].append(kvf[:, :, VB_OFF:].reshape(b_s, t_s, H_B, HEAD_DIM))

    return (xp.reshape(b_p, t_p, d), xs.reshape(b_s, t_s, d)) + tuple(jnp.stack(o) for o in outs)
```

```python
import functools
import math

import numpy as np
import jax
import jax.numpy as jnp
from jax import lax
from jax.experimental import pallas as pl
from jax.experimental.pallas import tpu as pltpu

CHUNK = 64
HEAD_DIM = 128
H_A = 8
H_A_KV = 2
WIN_CHUNKS = 2
WINDOW = WIN_CHUNKS * CHUNK
H_B = 4
BAND_CHUNKS = 8
BAND_ROWS = BAND_CHUNKS * CHUNK
REL_CLIP = 128
H_M = 4
ROT_DIM = HEAD_DIM // 4
ROPE_THETA = 500000.0
EPS = 1e-6
PAST_LEN = 1024
LOG2E = math.log2(math.e)
Q_SCALE = HEAD_DIM ** -0.5 * LOG2E

Q_COLS = (H_A + H_B + H_M) * HEAD_DIM
KV_COLS = (2 * H_A_KV + 2 * H_B) * HEAD_DIM
MEM_COLS = 2 * H_M * HEAD_DIM
QB_OFF = H_A * HEAD_DIM
QM_OFF = (H_A + H_B) * HEAD_DIM
KA_OFF = 0
VA_OFF = H_A_KV * HEAD_DIM
KB_OFF = 2 * H_A_KV * HEAD_DIM
VB_OFF = KB_OFF + H_B * HEAD_DIM

V7X_VMEM_LIMIT_BYTES = 58 * 1024 * 1024
HEAD_GROUP = 4
FF_TILE = 512
FFN_ROW_TILE = 1024
ROW_TILE = 512
PROMPT_Q_TILE = 256
MASK_VALUE = -1e30

_NT_DIMS = (((1,), (1,)), ((), ()))


def _row_tile(rows, preferred):
    tile = min(rows, preferred)
    while rows % tile:
        tile //= 2
    return tile


def _params(semantics):
    return pltpu.CompilerParams(dimension_semantics=semantics,
                                vmem_limit_bytes=V7X_VMEM_LIMIT_BYTES)


def _rms_rows(x, g):
    ms = jnp.mean(x * x, axis=-1, keepdims=True)
    return x * lax.rsqrt(ms + EPS) * g


def _layer_spec(shape, index_map):
    return pl.BlockSpec((None,) + tuple(shape), index_map)


def _ffn_kernel(x_ref, g_ref, wg_ref, wu_ref, wo_ref, o_ref, xn_ref, *, n_ff, last_cols):
    j = pl.program_id(1)

    def accumulate_onto(base_ref, cols):
        xn = xn_ref[...]
        gate = jnp.dot(xn, wg_ref[:, :cols], preferred_element_type=jnp.float32)
        up = jnp.dot(xn, wu_ref[:, :cols], preferred_element_type=jnp.float32)
        act = (gate * jax.nn.sigmoid(gate) * up * 0.5).astype(jnp.bfloat16)
        o_ref[...] = base_ref[...] + jnp.dot(act, wo_ref[:cols, :], preferred_element_type=jnp.float32)

    def normalise():
        xn_ref[...] = _rms_rows(x_ref[...], g_ref[...]).astype(xn_ref.dtype)

    @pl.when(j == 0)
    def _():
        normalise()
        accumulate_onto(x_ref, FF_TILE if n_ff > 1 else last_cols)

    if last_cols == FF_TILE:
        @pl.when(j != 0)
        def _():
            accumulate_onto(o_ref, FF_TILE)
    elif n_ff > 1:
        @pl.when((j != 0) & (j != n_ff - 1))
        def _():
            accumulate_onto(o_ref, FF_TILE)

        @pl.when(j == n_ff - 1)
        def _():
            accumulate_onto(o_ref, last_cols)


def _ffn(x, g, wg, wu, wo, layer):
    rows, d = x.shape
    f = wo.shape[1]
    n_ff = pl.cdiv(f, FF_TILE)
    tm = _row_tile(rows, FFN_ROW_TILE)
    return pl.pallas_call(
        functools.partial(_ffn_kernel, n_ff=n_ff, last_cols=f - (n_ff - 1) * FF_TILE),
        grid=(rows // tm, n_ff),
        in_specs=[
            pl.BlockSpec((tm, d), lambda i, j: (i, 0)),
            _layer_spec((1, d), lambda i, j: (layer, 0, 0)),
            _layer_spec((d, FF_TILE), lambda i, j: (layer, 0, j)),
            _layer_spec((d, FF_TILE), lambda i, j: (layer, 0, j)),
            _layer_spec((FF_TILE, d), lambda i, j: (layer, j, 0)),
        ],
        out_specs=pl.BlockSpec((tm, d), lambda i, j: (i, 0)),
        out_shape=jax.ShapeDtypeStruct((rows, d), jnp.float32),
        scratch_shapes=[pltpu.VMEM((tm, d), jnp.bfloat16)],
        compiler_params=_params(("parallel", "arbitrary")),
        name="ffn",
    )(x, g, wg, wu, wo)


def _prep_ffn_weights(w_i, w_o):
    f = w_o.shape[1]
    return (w_i[:, :, :f].astype(jnp.bfloat16), w_i[:, :, f:].astype(jnp.bfloat16),
            w_o.astype(jnp.bfloat16))


def _head_plan():
    plan = []
    for h in range(H_A):
        plan.append((0, True, True, h * HEAD_DIM))
    for h in range(H_A_KV):
        plan.append((1, True, False, KA_OFF + h * HEAD_DIM))
    for h in range(H_A_KV):
        plan.append((None, False, False, VA_OFF + h * HEAD_DIM))
    for h in range(H_B):
        plan.append((2, False, True, QB_OFF + h * HEAD_DIM))
    for h in range(H_B):
        plan.append((3, False, False, KB_OFF + h * HEAD_DIM))
    for h in range(H_B):
        plan.append((None, False, False, VB_OFF + h * HEAD_DIM))
    for h in range(H_M):
        plan.append((4, False, True, QM_OFF + h * HEAD_DIM))
    return plan


_PLAN = _head_plan()
IN_COLS = len(_PLAN) * HEAD_DIM


def _proj_kernel(x_ref, g_ref, w_ref, qkg_ref, cos_ref, sa_ref, sb_ref, q_ref, kvb_ref, kvf_ref):
    xn = _rms_rows(x_ref[...], g_ref[...]).astype(jnp.bfloat16)
    cos = cos_ref[...]
    sa = sa_ref[...]
    sb = sb_ref[...]
    gw = HEAD_GROUP * HEAD_DIM
    for grp in range(len(_PLAN) // HEAD_GROUP):
        p = jnp.dot(xn, w_ref[:, grp * gw:(grp + 1) * gw], preferred_element_type=jnp.float32)
        for hh in range(HEAD_GROUP):
            gain, rotary, is_query, col = _PLAN[grp * HEAD_GROUP + hh]
            z = p[:, hh * HEAD_DIM:(hh + 1) * HEAD_DIM]
            if gain is not None:
                z = _rms_rows(z, qkg_ref[gain:gain + 1, :])
            if rotary:
                z = (z * cos + pltpu.roll(z, HEAD_DIM - ROT_DIM // 2, 1) * sa
                     + pltpu.roll(z, ROT_DIM // 2, 1) * sb)
            cols = slice(col, col + HEAD_DIM)
            if is_query:
                q_ref[:, cols] = (z * Q_SCALE).astype(q_ref.dtype)
            else:
                kvf_ref[:, cols] = z
                kvb_ref[:, cols] = z.astype(kvb_ref.dtype)


def _rope_tables(pos):
    half = ROT_DIM // 2
    inv_freq = ROPE_THETA ** (-jnp.arange(half, dtype=jnp.float32) / half)
    ang = pos.astype(jnp.float32)[:, None] * inv_freq[None, :]
    cos = jnp.cos(ang)
    sin = jnp.sin(ang)
    t = pos.shape[0]
    ones = jnp.ones((t, HEAD_DIM - ROT_DIM), jnp.float32)
    cos_t = jnp.concatenate([cos, cos, ones], axis=1)
    sa_t = jnp.concatenate([-sin, jnp.zeros((t, HEAD_DIM - half), jnp.float32)], axis=1)
    sb_t = jnp.concatenate([jnp.zeros((t, half), jnp.float32), sin,
                            jnp.zeros((t, HEAD_DIM - ROT_DIM), jnp.float32)], axis=1)
    return cos_t, sa_t, sb_t


def _proj(x, g, w_in, qk_gain, tables, seq_len, layer):
    rows, d = x.shape
    tm = _row_tile(rows, ROW_TILE)
    if tm <= seq_len:
        assert seq_len % tm == 0
        per_seq = seq_len // tm
        tab_map = lambda i: (i % per_seq, 0)
    else:
        assert tm % seq_len == 0
        tables = tuple(jnp.tile(t, (tm // seq_len, 1)) for t in tables)
        tab_map = lambda i: (0, 0)
    tab_spec = pl.BlockSpec((tm, HEAD_DIM), tab_map)
    return pl.pallas_call(
        _proj_kernel,
        grid=(rows // tm,),
        in_specs=[
            pl.BlockSpec((tm, d), lambda i: (i, 0)),
            _layer_spec((1, d), lambda i: (layer, 0, 0)),
            _layer_spec((d, IN_COLS), lambda i: (layer, 0, 0)),
            _layer_spec(qk_gain.shape[1:], lambda i: (layer, 0, 0)),
            tab_spec, tab_spec, tab_spec,
        ],
        out_specs=[
            pl.BlockSpec((tm, Q_COLS), lambda i: (i, 0)),
            pl.BlockSpec((tm, KV_COLS), lambda i: (i, 0)),
            pl.BlockSpec((tm, KV_COLS), lambda i: (i, 0)),
        ],
        out_shape=[
            jax.ShapeDtypeStruct((rows, Q_COLS), jnp.bfloat16),
            jax.ShapeDtypeStruct((rows, KV_COLS), jnp.bfloat16),
            jax.ShapeDtypeStruct((rows, KV_COLS), jnp.float32),
        ],
        compiler_params=_params(("parallel",)),
        name="proj",
    )(x, g, w_in, qk_gain, *tables)


def _memkv_kernel(x_ref, g_ref, w_ref, qkg_ref, of_ref, ob_ref):
    xn = _rms_rows(x_ref[...], g_ref[...]).astype(jnp.bfloat16)
    gw = H_M * HEAD_DIM
    for part in range(2):
        p = jnp.dot(xn, w_ref[:, part * gw:(part + 1) * gw], preferred_element_type=jnp.float32)
        for h in range(H_M):
            z = p[:, h * HEAD_DIM:(h + 1) * HEAD_DIM]
            if part == 0:
                z = _rms_rows(z, qkg_ref[5:6, :])
            cols = slice(part * gw + h * HEAD_DIM, part * gw + (h + 1) * HEAD_DIM)
            of_ref[:, cols] = z
            ob_ref[:, cols] = z.astype(ob_ref.dtype)


def _memkv(mem, g, w_kv, qk_gain, layer):
    rows, d = mem.shape
    tm = _row_tile(rows, ROW_TILE)
    return pl.pallas_call(
        _memkv_kernel,
        grid=(rows // tm,),
        in_specs=[
            pl.BlockSpec((tm, d), lambda i: (i, 0)),
            _layer_spec((1, d), lambda i: (layer, 0, 0)),
            _layer_spec((d, MEM_COLS), lambda i: (layer, 0, 0)),
            _layer_spec(qk_gain.shape[1:], lambda i: (layer, 0, 0)),
        ],
        out_specs=[pl.BlockSpec((tm, MEM_COLS), lambda i: (i, 0))] * 2,
        out_shape=[jax.ShapeDtypeStruct((rows, MEM_COLS), jnp.float32),
                   jax.ShapeDtypeStruct((rows, MEM_COLS), jnp.bfloat16)],
        compiler_params=_params(("parallel",)),
        name="memkv",
    )(mem, g, w_kv, qk_gain)


def _attn_kernel(sink_ref, q_ref, kv_ref, mkv_ref, ba_ref, bb_ref, o_ref, *,
                 layer, q_tile, base, pieces_a, pieces_b):
    row0 = pl.multiple_of(pl.program_id(1) * q_tile, q_tile)

    def piece_rows(off, size):
        start = base + row0 + off
        if base + off < 0:
            start = jnp.maximum(start, 0)
        return pl.ds(pl.multiple_of(start, math.gcd(q_tile, abs(off), base)), size)

    rows_a = [piece_rows(off, size) for off, size in pieces_a]
    rows_b = [piece_rows(off, size) for off, size in pieces_b]
    rows_m = [pl.ds(0, mkv_ref.shape[1])]
    group = H_A // H_A_KV

    jobs = []
    for g in range(H_A_KV):
        heads = range(g * group, (g + 1) * group)
        jobs.append(([h * HEAD_DIM for h in heads], kv_ref, rows_a, KA_OFF + g * HEAD_DIM,
                     VA_OFF + g * HEAD_DIM, lambda: ba_ref[0], [sink_ref[layer, h] * LOG2E for h in heads]))
    for h in range(H_B):
        jobs.append(([QB_OFF + h * HEAD_DIM], kv_ref, rows_b, KB_OFF + h * HEAD_DIM,
                     VB_OFF + h * HEAD_DIM, functools.partial(lambda hh: bb_ref[0, hh], h), None))
    for h in range(H_M):
        jobs.append(([QM_OFF + h * HEAD_DIM], mkv_ref, rows_m, h * HEAD_DIM,
                     (H_M + h) * HEAD_DIM, None, None))

    scores = []
    for q_cols, k_ref, rows_list, k_col, _, _, _ in jobs:
        qs = [q_ref[0, :, c:c + HEAD_DIM] for c in q_cols]
        q = qs[0] if len(qs) == 1 else jnp.concatenate(qs, axis=0)
        parts = [lax.dot_general(q, k_ref[0, rows, k_col:k_col + HEAD_DIM], _NT_DIMS,
                                 preferred_element_type=jnp.float32) for rows in rows_list]
        scores.append(parts[0] if len(parts) == 1 else jnp.concatenate(parts, axis=1))

    probs = []
    dens = []
    for (q_cols, _, _, _, _, bias, sinks), s_all in zip(jobs, scores):
        es = []
        ds = []
        for i in range(len(q_cols)):
            s = s_all[i * q_tile:(i + 1) * q_tile]
            if bias is not None:
                s = s + bias()
            m = jnp.max(s, axis=-1, keepdims=True)
            if sinks is not None:
                m = jnp.maximum(m, sinks[i])
            e = jnp.exp2(s - m)
            den = jnp.sum(e, axis=-1, keepdims=True)
            if sinks is not None:
                den = den + jnp.exp2(sinks[i] - m)
            es.append(e.astype(jnp.bfloat16))
            ds.append(den)
        probs.append(es[0] if len(es) == 1 else jnp.concatenate(es, axis=0))
        dens.append(ds)

    for (q_cols, k_ref, rows_list, _, v_col, _, _), e, ds in zip(jobs, probs, dens):
        acc = None
        lo = 0
        for rows in rows_list:
            pv = jnp.dot(e[:, lo:lo + rows.size], k_ref[0, rows, v_col:v_col + HEAD_DIM],
                         preferred_element_type=jnp.float32)
            acc = pv if acc is None else acc + pv
            lo += rows.size
        for i, c in enumerate(q_cols):
            o = acc[i * q_tile:(i + 1) * q_tile] / ds[i]
            o_ref[0, :, c:c + HEAD_DIM] = o.astype(o_ref.dtype)


def _window_pieces(past_rows, q_tile, base):
    size = past_rows if base >= past_rows else math.gcd(past_rows, q_tile)
    pieces = [(-past_rows + k * size, size) for k in range(past_rows // size)]
    return pieces + [(0, q_tile)]


def _band_masks(q_tile, past_rows, n_past_chunks, n_tiles, first_real, new_rows):
    win = past_rows + q_tile
    r = np.arange(q_tile)[:, None] // CHUNK
    w = np.arange(win)[None, :]
    band = (w // CHUNK >= r) & (w // CHUNK <= r + n_past_chunks)
    masks = []
    for i in range(n_tiles):
        row = i * q_tile + w - past_rows
        masks.append(band & (row >= first_real) & (row < new_rows))
    n_var = n_tiles
    while n_var > 1 and np.array_equal(masks[n_var - 2], masks[n_tiles - 1]):
        n_var -= 1
    return np.stack(masks[:n_var])


def _rel_bias_tiles(rel_tab, q_tile):
    win = BAND_ROWS + q_tile
    n = win + q_tile
    k = np.arange(n)
    dist = np.where(k < win, BAND_ROWS - k, BAND_ROWS + n - k)
    seq = rel_tab[:, :, np.clip(dist, -REL_CLIP, REL_CLIP) + REL_CLIP].astype(jnp.float32) * LOG2E
    flat = jnp.tile(seq, (1, 1, q_tile))[:, :, :q_tile * (n - 1)]
    return flat.reshape(rel_tab.shape[0], H_B, q_tile, n - 1)[..., :win]


def _attn_biases(rel_tab, q_tile, n_tiles, past_a, past_b, new_rows):
    mask_a = _band_masks(q_tile, WINDOW, WIN_CHUNKS, n_tiles, -past_a, new_rows)
    mask_b = _band_masks(q_tile, BAND_ROWS, BAND_CHUNKS, n_tiles, -past_b, new_rows)
    bias_a = jnp.asarray(np.where(mask_a, 0.0, MASK_VALUE), jnp.float32)
    rel = _rel_bias_tiles(rel_tab, q_tile)
    bias_b = jnp.where(jnp.asarray(mask_b)[None, :, None], rel[:, None], MASK_VALUE)
    return bias_a, bias_b


def _attention(q, kv, mkv, sinks, bias_a, bias_b, q_tile, base, layer):
    b, tq, _ = q.shape
    assert kv.shape[1] == base + tq
    n_a = bias_a.shape[0]
    n_b = bias_b.shape[1]
    win_a = WINDOW + q_tile
    win_b = BAND_ROWS + q_tile
    kern = functools.partial(_attn_kernel, layer=layer, q_tile=q_tile, base=base,
                             pieces_a=_window_pieces(WINDOW, q_tile, base),
                             pieces_b=_window_pieces(BAND_ROWS, q_tile, base))
    return pl.pallas_call(
        kern,
        grid=(b, tq // q_tile),
        in_specs=[
            pl.BlockSpec(memory_space=pltpu.SMEM),
            pl.BlockSpec((1, q_tile, Q_COLS), lambda bi, i: (bi, i, 0)),
            pl.BlockSpec((1,) + kv.shape[1:], lambda bi, i: (bi, 0, 0)),
            pl.BlockSpec((1,) + mkv.shape[1:], lambda bi, i: (bi, 0, 0)),
            pl.BlockSpec((1, q_tile, win_a), lambda bi, i: (jnp.minimum(i, n_a - 1), 0, 0)),
            _layer_spec((1, H_B, q_tile, win_b),
                        lambda bi, i: (layer, jnp.minimum(i, n_b - 1), 0, 0, 0)),
        ],
        out_specs=pl.BlockSpec((1, q_tile, Q_COLS), lambda bi, i: (bi, i, 0)),
        out_shape=jax.ShapeDtypeStruct((b, tq, Q_COLS), jnp.bfloat16),
        compiler_params=_params(("parallel", "arbitrary")),
        name="attn",
    )(sinks, q, kv, mkv, bias_a, bias_b)


def _outproj_kernel(x_ref, o_ref, w_ref, y_ref):
    y_ref[...] = x_ref[...] + jnp.dot(o_ref[...], w_ref[...], preferred_element_type=jnp.float32)


def _outproj(x, o, w, layer):
    rows, d = x.shape
    tm = _row_tile(rows, ROW_TILE)
    return pl.pallas_call(
        _outproj_kernel,
        grid=(rows // tm,),
        in_specs=[
            pl.BlockSpec((tm, d), lambda i: (i, 0)),
            pl.BlockSpec((tm, o.shape[1]), lambda i: (i, 0)),
            _layer_spec(w.shape[1:], lambda i: (layer, 0, 0)),
        ],
        out_specs=pl.BlockSpec((tm, d), lambda i: (i, 0)),
        out_shape=jax.ShapeDtypeStruct((rows, d), jnp.float32),
        compiler_params=_params(("parallel",)),
        name="outproj",
    )(x, o, w)


def kernel(x_prompt, x_sample, cache_a_k, cache_a_v, cache_b_k, cache_b_v, cache_mem_k, cache_mem_v,
           mem_prompt, norm_ff1, w_ff1_in, w_ff1_out, norm_mix, w_in, qk_gain, attn_sinks, rel_bias,
           norm_mem, w_mem_kv, w_out, norm_ff2, w_ff2_in, w_ff2_out):
    b_p, t_p, d = x_prompt.shape
    b_s, t_s, _ = x_sample.shape
    depth = w_in.shape[0]
    n_mem = mem_prompt.shape[1]
    bf16 = jnp.bfloat16
    assert t_p % PROMPT_Q_TILE == 0 and t_s <= CHUNK

    w1g, w1u, w1o = _prep_ffn_weights(w_ff1_in, w_ff1_out)
    w2g, w2u, w2o = _prep_ffn_weights(w_ff2_in, w_ff2_out)
    w_in_b = w_in.astype(bf16)
    w_out_b = w_out.astype(bf16)
    w_mem_b = w_mem_kv.astype(bf16)
    g_ff1 = norm_ff1[:, None, :]
    g_ff2 = norm_ff2[:, None, :]
    g_mix = norm_mix[:, None, :]
    g_mem = norm_mem[:, None, :]

    tab_p = _rope_tables(jnp.arange(t_p, dtype=jnp.int32))
    tab_s = _rope_tables(PAST_LEN + jnp.arange(t_s, dtype=jnp.int32))
    keep_a = min(WINDOW, t_p)
    keep_b = min(BAND_ROWS, t_p)
    la = cache_a_k.shape[2]
    lb = cache_b_k.shape[2]

    bias_a_p, bias_b_p = _attn_biases(rel_bias, PROMPT_Q_TILE, t_p // PROMPT_Q_TILE, 0, 0, t_p)
    bias_a_s, bias_b_s = _attn_biases(rel_bias, CHUNK, 1, la, lb, t_s)

    def heads_flat(c):
        return c.reshape(c.shape[:3] + (-1,)).astype(bf16)

    def front(c, rows):
        return jnp.pad(c, ((0, 0), (0, 0), (rows - c.shape[2], 0), (0, 0)))

    past_kv = jnp.concatenate(
        [front(heads_flat(cache_a_k), BAND_ROWS), front(heads_flat(cache_a_v), BAND_ROWS),
         front(heads_flat(cache_b_k), BAND_ROWS), front(heads_flat(cache_b_v), BAND_ROWS)], axis=3)
    mkv_s = jnp.concatenate([heads_flat(cache_mem_k), heads_flat(cache_mem_v)], axis=3)

    xp = x_prompt.reshape(b_p * t_p, d)
    xs = x_sample.reshape(b_s * t_s, d)
    mem_flat = mem_prompt.reshape(b_p * n_mem, d)
    outs = [[] for _ in range(10)]
    for l in range(depth):
        mkv_f, mkv_b = _memkv(mem_flat, g_mem, w_mem_b, qk_gain, l)
        xp = _ffn(xp, g_ff1, w1g, w1u, w1o, l)
        q, kvb, kvf = _proj(xp, g_mix, w_in_b, qk_gain, tab_p, t_p, l)
        o = _attention(q.reshape(b_p, t_p, Q_COLS), kvb.reshape(b_p, t_p, KV_COLS),
                       mkv_b.reshape(b_p, n_mem, MEM_COLS), attn_sinks, bias_a_p, bias_b_p,
                       PROMPT_Q_TILE, 0, l)
        xp = _outproj(xp, o.reshape(b_p * t_p, Q_COLS), w_out_b, l)
        xp = _ffn(xp, g_ff2, w2g, w2u, w2o, l)
        kvf = kvf.reshape(b_p, t_p, KV_COLS)
        mkv_f = mkv_f.reshape(b_p, n_mem, MEM_COLS)
        outs[0].append(kvf[:, t_p - keep_a:, KA_OFF:VA_OFF].reshape(b_p, keep_a, H_A_KV, HEAD_DIM))
        outs[1].append(kvf[:, t_p - keep_a:, VA_OFF:KB_OFF].reshape(b_p, keep_a, H_A_KV, HEAD_DIM))
        outs[2].append(kvf[:, t_p - keep_b:, KB_OFF:VB_OFF].reshape(b_p, keep_b, H_B, HEAD_DIM))
        outs[3].append(kvf[:, t_p - keep_b:, VB_OFF:].reshape(b_p, keep_b, H_B, HEAD_DIM))
        outs[4].append(mkv_f[:, :, :H_M * HEAD_DIM].reshape(b_p, n_mem, H_M, HEAD_DIM))
        outs[5].append(mkv_f[:, :, H_M * HEAD_DIM:].reshape(b_p, n_mem, H_M, HEAD_DIM))

        xs = _ffn(xs, g_ff1, w1g, w1u, w1o, l)
        q, kvb, kvf = _proj(xs, g_mix, w_in_b, qk_gain, tab_s, t_s, l)
        q = jnp.pad(q.reshape(b_s, t_s, Q_COLS), ((0, 0), (0, CHUNK - t_s), (0, 0)))
        kv = jnp.concatenate(
            [past_kv[l], kvb.reshape(b_s, t_s, KV_COLS), jnp.zeros((b_s, CHUNK - t_s, KV_COLS), bf16)],
            axis=1)
        o = _attention(q, kv, mkv_s[l], attn_sinks, bias_a_s, bias_b_s, CHUNK, BAND_ROWS, l)
        xs = _outproj(xs, o[:, :t_s].reshape(b_s * t_s, Q_COLS), w_out_b, l)
        xs = _ffn(xs, g_ff2, w2g, w2u, w2o, l)
        kvf = kvf.reshape(b_s, t_s, KV_COLS)
        outs[6].append(kvf[:, :, KA_OFF:VA_OFF].reshape(b_s, t_s, H_A_KV, HEAD_DIM))
        outs[7].append(kvf[:, :, VA_OFF:KB_OFF].reshape(b_s, t_s, H_A_KV, HEAD_DIM))
        outs[8].append(kvf[:, :, KB_OFF:VB_OFF].reshape(b_s, t_s, H_B, HEAD_DIM))
        outs[9].append(kvf[:, :, VB_OFF:].reshape(b_s, t_s, H_B, HEAD_DIM))

    return (xp.reshape(b_p, t_p, d), xs.reshape(b_s, t_s, d)) + tuple(jnp.stack(o) for o in outs)
```

```python
import functools
import math

import numpy as np
import jax
import jax.numpy as jnp
from jax import lax
from jax.experimental import pallas as pl
from jax.experimental.pallas import tpu as pltpu

CHUNK = 64
HEAD_DIM = 128
H_A = 8
H_A_KV = 2
WIN_CHUNKS = 2
WINDOW = WIN_CHUNKS * CHUNK
H_B = 4
BAND_CHUNKS = 8
BAND_ROWS = BAND_CHUNKS * CHUNK
REL_CLIP = 128
H_M = 4
ROT_DIM = HEAD_DIM // 4
ROPE_THETA = 500000.0
EPS = 1e-6
PAST_LEN = 1024
LOG2E = math.log2(math.e)
Q_SCALE = HEAD_DIM ** -0.5 * LOG2E

Q_COLS = (H_A + H_B + H_M) * HEAD_DIM
KV_COLS = (2 * H_A_KV + 2 * H_B) * HEAD_DIM
MEM_COLS = 2 * H_M * HEAD_DIM
QB_OFF = H_A * HEAD_DIM
QM_OFF = (H_A + H_B) * HEAD_DIM
KA_OFF = 0
VA_OFF = H_A_KV * HEAD_DIM
KB_OFF = 2 * H_A_KV * HEAD_DIM
VB_OFF = KB_OFF + H_B * HEAD_DIM

V7X_VMEM_LIMIT_BYTES = 58 * 1024 * 1024
HEAD_GROUP = 4
FF_TILE = 512
FFN_ROW_TILE = 1024
V7X_LANES = 128
CAST_ROW_TILE = 256
ROW_TILE = 512
PROMPT_Q_TILE = 256
MASK_VALUE = -1e30

_NT_DIMS = (((1,), (1,)), ((), ()))


def _row_tile(rows, preferred):
    tile = min(rows, preferred)
    while rows % tile:
        tile //= 2
    return tile


def _params(semantics):
    return pltpu.CompilerParams(dimension_semantics=semantics,
                                vmem_limit_bytes=V7X_VMEM_LIMIT_BYTES)


def _rms_rows(x, g):
    ms = jnp.mean(x * x, axis=-1, keepdims=True)
    return x * lax.rsqrt(ms + EPS) * g


def _layer_spec(shape, index_map):
    return pl.BlockSpec((None,) + tuple(shape), index_map)


def _ffn_kernel(x_ref, g_ref, wg_ref, wu_ref, wo_ref, o_ref, xn_ref, *, n_ff, last_cols):
    j = pl.program_id(1)

    def accumulate_onto(base_ref, cols):
        xn = xn_ref[...]
        gate = jnp.dot(xn, wg_ref[:, :cols], preferred_element_type=jnp.float32)
        up = jnp.dot(xn, wu_ref[:, :cols], preferred_element_type=jnp.float32)
        act = (gate * jax.nn.sigmoid(gate) * up * 0.5).astype(jnp.bfloat16)
        o_ref[...] = base_ref[...] + jnp.dot(act, wo_ref[:cols, :], preferred_element_type=jnp.float32)

    def normalise():
        xn_ref[...] = _rms_rows(x_ref[...], g_ref[...]).astype(xn_ref.dtype)

    @pl.when(j == 0)
    def _():
        normalise()
        accumulate_onto(x_ref, FF_TILE if n_ff > 1 else last_cols)

    if last_cols == FF_TILE:
        @pl.when(j != 0)
        def _():
            accumulate_onto(o_ref, FF_TILE)
    elif n_ff > 1:
        @pl.when((j != 0) & (j != n_ff - 1))
        def _():
            accumulate_onto(o_ref, FF_TILE)

        @pl.when(j == n_ff - 1)
        def _():
            accumulate_onto(o_ref, last_cols)


def _ffn(x, g, wgu, wo, layer):
    rows, d = x.shape
    f = wo.shape[1]
    n_ff = pl.cdiv(f, FF_TILE)
    tm = _row_tile(rows, FFN_ROW_TILE)
    return pl.pallas_call(
        functools.partial(_ffn_kernel, n_ff=n_ff, last_cols=f - (n_ff - 1) * FF_TILE),
        grid=(rows // tm, n_ff),
        in_specs=[
            pl.BlockSpec((tm, d), lambda i, j: (i, 0)),
            _layer_spec((1, d), lambda i, j: (layer, 0, 0)),
            pl.BlockSpec((None, None, d, FF_TILE), lambda i, j: (0, layer, 0, j)),
            pl.BlockSpec((None, None, d, FF_TILE), lambda i, j: (1, layer, 0, j)),
            _layer_spec((FF_TILE, d), lambda i, j: (layer, j, 0)),
        ],
        out_specs=pl.BlockSpec((tm, d), lambda i, j: (i, 0)),
        out_shape=jax.ShapeDtypeStruct((rows, d), jnp.float32),
        scratch_shapes=[pltpu.VMEM((tm, d), jnp.bfloat16)],
        compiler_params=_params(("parallel", "arbitrary")),
        name="ffn",
    )(x, g, wgu, wgu, wo)


def _split_cast_kernel(w_ref, o_ref):
    o_ref[...] = w_ref[...].astype(o_ref.dtype)


def _split_cast(w_i):
    depth, d, two_f = w_i.shape
    f = two_f // 2
    assert f % V7X_LANES == 0
    rt = _row_tile(d, CAST_ROW_TILE)
    return pl.pallas_call(
        _split_cast_kernel,
        grid=(depth, 2, d // rt),
        in_specs=[_layer_spec((rt, f), lambda l, p, r: (l, r, p))],
        out_specs=pl.BlockSpec((None, None, rt, f), lambda l, p, r: (p, l, r, 0)),
        out_shape=jax.ShapeDtypeStruct((2, depth, d, f), jnp.bfloat16),
        compiler_params=_params(("parallel", "parallel", "parallel")),
        name="split_cast",
    )(w_i)


def _head_plan():
    plan = []
    for h in range(H_A):
        plan.append((0, True, True, h * HEAD_DIM))
    for h in range(H_A_KV):
        plan.append((1, True, False, KA_OFF + h * HEAD_DIM))
    for h in range(H_A_KV):
        plan.append((None, False, False, VA_OFF + h * HEAD_DIM))
    for h in range(H_B):
        plan.append((2, False, True, QB_OFF + h * HEAD_DIM))
    for h in range(H_B):
        plan.append((3, False, False, KB_OFF + h * HEAD_DIM))
    for h in range(H_B):
        plan.append((None, False, False, VB_OFF + h * HEAD_DIM))
    for h in range(H_M):
        plan.append((4, False, True, QM_OFF + h * HEAD_DIM))
    return plan


_PLAN = _head_plan()
IN_COLS = len(_PLAN) * HEAD_DIM


def _proj_kernel(x_ref, g_ref, w_ref, qkg_ref, cos_ref, sa_ref, sb_ref, q_ref, kvb_ref, kvf_ref):
    xn = _rms_rows(x_ref[...], g_ref[...]).astype(jnp.bfloat16)
    cos = cos_ref[...]
    sa = sa_ref[...]
    sb = sb_ref[...]
    gw = HEAD_GROUP * HEAD_DIM
    for grp in range(len(_PLAN) // HEAD_GROUP):
        p = jnp.dot(xn, w_ref[:, grp * gw:(grp + 1) * gw], preferred_element_type=jnp.float32)
        for hh in range(HEAD_GROUP):
            gain, rotary, is_query, col = _PLAN[grp * HEAD_GROUP + hh]
            z = p[:, hh * HEAD_DIM:(hh + 1) * HEAD_DIM]
            if gain is not None:
                z = _rms_rows(z, qkg_ref[gain:gain + 1, :])
            if rotary:
                z = (z * cos + pltpu.roll(z, HEAD_DIM - ROT_DIM // 2, 1) * sa
                     + pltpu.roll(z, ROT_DIM // 2, 1) * sb)
            cols = slice(col, col + HEAD_DIM)
            if is_query:
                q_ref[:, cols] = (z * Q_SCALE).astype(q_ref.dtype)
            else:
                kvf_ref[:, cols] = z
                kvb_ref[:, cols] = z.astype(kvb_ref.dtype)


def _rope_tables(pos):
    half = ROT_DIM // 2
    inv_freq = ROPE_THETA ** (-jnp.arange(half, dtype=jnp.float32) / half)
    ang = pos.astype(jnp.float32)[:, None] * inv_freq[None, :]
    cos = jnp.cos(ang)
    sin = jnp.sin(ang)
    t = pos.shape[0]
    ones = jnp.ones((t, HEAD_DIM - ROT_DIM), jnp.float32)
    cos_t = jnp.concatenate([cos, cos, ones], axis=1)
    sa_t = jnp.concatenate([-sin, jnp.zeros((t, HEAD_DIM - half), jnp.float32)], axis=1)
    sb_t = jnp.concatenate([jnp.zeros((t, half), jnp.float32), sin,
                            jnp.zeros((t, HEAD_DIM - ROT_DIM), jnp.float32)], axis=1)
    return cos_t, sa_t, sb_t


def _proj(x, g, w_in, qk_gain, tables, seq_len, layer):
    rows, d = x.shape
    tm = _row_tile(rows, ROW_TILE)
    if tm <= seq_len:
        assert seq_len % tm == 0
        per_seq = seq_len // tm
        tab_map = lambda i: (i % per_seq, 0)
    else:
        assert tm % seq_len == 0
        tables = tuple(jnp.tile(t, (tm // seq_len, 1)) for t in tables)
        tab_map = lambda i: (0, 0)
    tab_spec = pl.BlockSpec((tm, HEAD_DIM), tab_map)
    return pl.pallas_call(
        _proj_kernel,
        grid=(rows // tm,),
        in_specs=[
            pl.BlockSpec((tm, d), lambda i: (i, 0)),
            _layer_spec((1, d), lambda i: (layer, 0, 0)),
            _layer_spec((d, IN_COLS), lambda i: (layer, 0, 0)),
            _layer_spec(qk_gain.shape[1:], lambda i: (layer, 0, 0)),
            tab_spec, tab_spec, tab_spec,
        ],
        out_specs=[
            pl.BlockSpec((tm, Q_COLS), lambda i: (i, 0)),
            pl.BlockSpec((tm, KV_COLS), lambda i: (i, 0)),
            pl.BlockSpec((tm, KV_COLS), lambda i: (i, 0)),
        ],
        out_shape=[
            jax.ShapeDtypeStruct((rows, Q_COLS), jnp.bfloat16),
            jax.ShapeDtypeStruct((rows, KV_COLS), jnp.bfloat16),
            jax.ShapeDtypeStruct((rows, KV_COLS), jnp.float32),
        ],
        compiler_params=_params(("parallel",)),
        name="proj",
    )(x, g, w_in, qk_gain, *tables)


def _memkv_kernel(x_ref, g_ref, w_ref, qkg_ref, of_ref, ob_ref):
    xn = _rms_rows(x_ref[...], g_ref[...]).astype(jnp.bfloat16)
    gw = H_M * HEAD_DIM
    for part in range(2):
        p = jnp.dot(xn, w_ref[:, part * gw:(part + 1) * gw], preferred_element_type=jnp.float32)
        for h in range(H_M):
            z = p[:, h * HEAD_DIM:(h + 1) * HEAD_DIM]
            if part == 0:
                z = _rms_rows(z, qkg_ref[5:6, :])
            cols = slice(part * gw + h * HEAD_DIM, part * gw + (h + 1) * HEAD_DIM)
            of_ref[:, cols] = z
            ob_ref[:, cols] = z.astype(ob_ref.dtype)


def _memkv(mem, g, w_kv, qk_gain, layer):
    rows, d = mem.shape
    tm = _row_tile(rows, ROW_TILE)
    return pl.pallas_call(
        _memkv_kernel,
        grid=(rows // tm,),
        in_specs=[
            pl.BlockSpec((tm, d), lambda i: (i, 0)),
            _layer_spec((1, d), lambda i: (layer, 0, 0)),
            _layer_spec((d, MEM_COLS), lambda i: (layer, 0, 0)),
            _layer_spec(qk_gain.shape[1:], lambda i: (layer, 0, 0)),
        ],
        out_specs=[pl.BlockSpec((tm, MEM_COLS), lambda i: (i, 0))] * 2,
        out_shape=[jax.ShapeDtypeStruct((rows, MEM_COLS), jnp.float32),
                   jax.ShapeDtypeStruct((rows, MEM_COLS), jnp.bfloat16)],
        compiler_params=_params(("parallel",)),
        name="memkv",
    )(mem, g, w_kv, qk_gain)


def _attn_kernel(sink_ref, q_ref, kv_ref, mkv_ref, ba_ref, bb_ref, o_ref, *,
                 layer, q_tile, base, sub_a, pieces_b):
    tile = pl.program_id(1)
    row0 = pl.multiple_of(tile * q_tile, q_tile)
    n_sub = q_tile // sub_a
    n_var_a = ba_ref.shape[0]
    group = H_A // H_A_KV

    def piece_rows(off, size):
        start = base + row0 + off
        if base + off < 0:
            start = jnp.maximum(start, 0)
        return pl.ds(pl.multiple_of(start, math.gcd(q_tile, abs(off), base)), size)

    def window_a(u):
        start = base + row0 + u * sub_a - WINDOW
        if base + u * sub_a - WINDOW < 0:
            start = jnp.maximum(start, 0)
        return pl.ds(pl.multiple_of(start, math.gcd(sub_a, base)), WINDOW + sub_a)

    def bias_a(u):
        return ba_ref[jnp.minimum(tile * n_sub + u, n_var_a - 1)]

    jobs = []
    for g in range(H_A_KV):
        heads = range(g * group, (g + 1) * group)
        sinks = [sink_ref[layer, h] * LOG2E for h in heads]
        for u in range(n_sub):
            jobs.append((slice(u * sub_a, (u + 1) * sub_a), [h * HEAD_DIM for h in heads], kv_ref,
                         [window_a(u)], KA_OFF + g * HEAD_DIM, VA_OFF + g * HEAD_DIM,
                         functools.partial(bias_a, u), sinks))
    rows_b = [piece_rows(off, size) for off, size in pieces_b]
    for h in range(H_B):
        jobs.append((slice(0, q_tile), [QB_OFF + h * HEAD_DIM], kv_ref, rows_b, KB_OFF + h * HEAD_DIM,
                     VB_OFF + h * HEAD_DIM, functools.partial(lambda hh: bb_ref[0, hh], h), None))
    rows_m = [pl.ds(0, mkv_ref.shape[1])]
    for h in range(H_M):
        jobs.append((slice(0, q_tile), [QM_OFF + h * HEAD_DIM], mkv_ref, rows_m, h * HEAD_DIM,
                     (H_M + h) * HEAD_DIM, None, None))

    scores = []
    for q_rows, q_cols, k_ref, rows_list, k_col, _, _, _ in jobs:
        qs = [q_ref[0, q_rows, c:c + HEAD_DIM] for c in q_cols]
        q = qs[0] if len(qs) == 1 else jnp.concatenate(qs, axis=0)
        parts = [lax.dot_general(q, k_ref[0, rows, k_col:k_col + HEAD_DIM], _NT_DIMS,
                                 preferred_element_type=jnp.float32) for rows in rows_list]
        scores.append(parts[0] if len(parts) == 1 else jnp.concatenate(parts, axis=1))

    probs = []
    dens = []
    for (q_rows, q_cols, _, _, _, _, bias, sinks), s_all in zip(jobs, scores):
        n_rows = q_rows.stop - q_rows.start
        bias_val = None if bias is None else bias()
        es = []
        ds = []
        for i in range(len(q_cols)):
            s = s_all[i * n_rows:(i + 1) * n_rows]
            if bias_val is not None:
                s = s + bias_val
            m = jnp.max(s, axis=-1, keepdims=True)
            if sinks is not None:
                m = jnp.maximum(m, sinks[i])
            e = jnp.exp2(s - m)
            den = jnp.sum(e, axis=-1, keepdims=True)
            if sinks is not None:
                den = den + jnp.exp2(sinks[i] - m)
            es.append(e.astype(jnp.bfloat16))
            ds.append(den)
        probs.append(es[0] if len(es) == 1 else jnp.concatenate(es, axis=0))
        dens.append(ds)

    for (q_rows, q_cols, k_ref, rows_list, _, v_col, _, _), e, ds in zip(jobs, probs, dens):
        n_rows = q_rows.stop - q_rows.start
        acc = None
        lo = 0
        for rows in rows_list:
            pv = jnp.dot(e[:, lo:lo + rows.size], k_ref[0, rows, v_col:v_col + HEAD_DIM],
                         preferred_element_type=jnp.float32)
            acc = pv if acc is None else acc + pv
            lo += rows.size
        for i, c in enumerate(q_cols):
            o = acc[i * n_rows:(i + 1) * n_rows] / ds[i]
            o_ref[0, q_rows, c:c + HEAD_DIM] = o.astype(o_ref.dtype)


def _window_pieces(past_rows, q_tile, base):
    size = past_rows if base >= past_rows else math.gcd(past_rows, q_tile)
    pieces = [(-past_rows + k * size, size) for k in range(past_rows // size)]
    return pieces + [(0, q_tile)]


def _leading_variants(masks):
    n_var = len(masks)
    while n_var > 1 and np.array_equal(masks[n_var - 2], masks[-1]):
        n_var -= 1
    return np.stack(masks[:n_var])


def _band_masks(q_tile, past_rows, n_past_chunks, n_tiles, first_real, new_rows):
    win = past_rows + q_tile
    r = np.arange(q_tile)[:, None] // CHUNK
    w = np.arange(win)[None, :]
    band = (w // CHUNK >= r) & (w // CHUNK <= r + n_past_chunks)
    masks = []
    for i in range(n_tiles):
        row = i * q_tile + w - past_rows
        masks.append(band & (row >= first_real) & (row < new_rows))
    return _leading_variants(masks)


def _window_masks_a(sub, n_sub_tiles, base, first_real, new_rows):
    masks = []
    for s in range(n_sub_tiles):
        q_row = s * sub + np.arange(sub)[:, None]
        k_row = max(s * sub - WINDOW, -base) + np.arange(WINDOW + sub)[None, :]
        q_chunk = q_row // CHUNK
        k_chunk = np.floor_divide(k_row, CHUNK)
        masks.append((k_chunk >= q_chunk - WIN_CHUNKS) & (k_chunk <= q_chunk)
                     & (k_row >= first_real) & (k_row < new_rows))
    return _leading_variants(masks)


def _rel_bias_tiles(rel_tab, q_tile):
    win = BAND_ROWS + q_tile
    n = win + q_tile
    k = np.arange(n)
    dist = np.where(k < win, BAND_ROWS - k, BAND_ROWS + n - k)
    seq = rel_tab[:, :, np.clip(dist, -REL_CLIP, REL_CLIP) + REL_CLIP].astype(jnp.float32) * LOG2E
    flat = jnp.tile(seq, (1, 1, q_tile))[:, :, :q_tile * (n - 1)]
    return flat.reshape(rel_tab.shape[0], H_B, q_tile, n - 1)[..., :win]


def _attn_biases(rel_tab, q_tile, n_tiles, base, past_a, past_b, new_rows):
    sub = min(q_tile, WINDOW)
    mask_a = _window_masks_a(sub, n_tiles * (q_tile // sub), base, -past_a, new_rows)
    mask_b = _band_masks(q_tile, BAND_ROWS, BAND_CHUNKS, n_tiles, -past_b, new_rows)
    bias_a = jnp.asarray(np.where(mask_a, 0.0, MASK_VALUE), jnp.float32)
    rel = _rel_bias_tiles(rel_tab, q_tile)
    bias_b = jnp.where(jnp.asarray(mask_b)[None, :, None], rel[:, None], MASK_VALUE)
    return bias_a, bias_b


def _attention(q, kv, mkv, sinks, bias_a, bias_b, q_tile, base, layer):
    b, tq, _ = q.shape
    assert kv.shape[1] == base + tq
    n_b = bias_b.shape[1]
    win_b = BAND_ROWS + q_tile
    kern = functools.partial(_attn_kernel, layer=layer, q_tile=q_tile, base=base,
                             sub_a=bias_a.shape[1],
                             pieces_b=_window_pieces(BAND_ROWS, q_tile, base))
    return pl.pallas_call(
        kern,
        grid=(b, tq // q_tile),
        in_specs=[
            pl.BlockSpec(memory_space=pltpu.SMEM),
            pl.BlockSpec((1, q_tile, Q_COLS), lambda bi, i: (bi, i, 0)),
            pl.BlockSpec((1,) + kv.shape[1:], lambda bi, i: (bi, 0, 0)),
            pl.BlockSpec((1,) + mkv.shape[1:], lambda bi, i: (bi, 0, 0)),
            pl.BlockSpec(bias_a.shape, lambda bi, i: (0, 0, 0)),
            _layer_spec((1, H_B, q_tile, win_b),
                        lambda bi, i: (layer, jnp.minimum(i, n_b - 1), 0, 0, 0)),
        ],
        out_specs=pl.BlockSpec((1, q_tile, Q_COLS), lambda bi, i: (bi, i, 0)),
        out_shape=jax.ShapeDtypeStruct((b, tq, Q_COLS), jnp.bfloat16),
        compiler_params=_params(("parallel", "arbitrary")),
        name="attn",
    )(sinks, q, kv, mkv, bias_a, bias_b)


def _outproj_kernel(x_ref, o_ref, w_ref, y_ref):
    y_ref[...] = x_ref[...] + jnp.dot(o_ref[...], w_ref[...], preferred_element_type=jnp.float32)


def _outproj(x, o, w, layer):
    rows, d = x.shape
    tm = _row_tile(rows, ROW_TILE)
    return pl.pallas_call(
        _outproj_kernel,
        grid=(rows // tm,),
        in_specs=[
            pl.BlockSpec((tm, d), lambda i: (i, 0)),
            pl.BlockSpec((tm, o.shape[1]), lambda i: (i, 0)),
            _layer_spec(w.shape[1:], lambda i: (layer, 0, 0)),
        ],
        out_specs=pl.BlockSpec((tm, d), lambda i: (i, 0)),
        out_shape=jax.ShapeDtypeStruct((rows, d), jnp.float32),
        compiler_params=_params(("parallel",)),
        name="outproj",
    )(x, o, w)


def kernel(x_prompt, x_sample, cache_a_k, cache_a_v, cache_b_k, cache_b_v, cache_mem_k, cache_mem_v,
           mem_prompt, norm_ff1, w_ff1_in, w_ff1_out, norm_mix, w_in, qk_gain, attn_sinks, rel_bias,
           norm_mem, w_mem_kv, w_out, norm_ff2, w_ff2_in, w_ff2_out):
    b_p, t_p, d = x_prompt.shape
    b_s, t_s, _ = x_sample.shape
    depth = w_in.shape[0]
    n_mem = mem_prompt.shape[1]
    bf16 = jnp.bfloat16
    assert t_p % PROMPT_Q_TILE == 0 and t_s <= CHUNK

    w1gu = _split_cast(w_ff1_in)
    w2gu = _split_cast(w_ff2_in)
    w1o = w_ff1_out.astype(bf16)
    w2o = w_ff2_out.astype(bf16)
    w_in_b = w_in.astype(bf16)
    w_out_b = w_out.astype(bf16)
    w_mem_b = w_mem_kv.astype(bf16)
    g_ff1 = norm_ff1[:, None, :]
    g_ff2 = norm_ff2[:, None, :]
    g_mix = norm_mix[:, None, :]
    g_mem = norm_mem[:, None, :]

    tab_p = _rope_tables(jnp.arange(t_p, dtype=jnp.int32))
    tab_s = _rope_tables(PAST_LEN + jnp.arange(t_s, dtype=jnp.int32))
    keep_a = min(WINDOW, t_p)
    keep_b = min(BAND_ROWS, t_p)
    la = cache_a_k.shape[2]
    lb = cache_b_k.shape[2]

    bias_a_p, bias_b_p = _attn_biases(rel_bias, PROMPT_Q_TILE, t_p // PROMPT_Q_TILE, 0, 0, 0, t_p)
    bias_a_s, bias_b_s = _attn_biases(rel_bias, CHUNK, 1, BAND_ROWS, la, lb, t_s)

    def heads_flat(c):
        return c.reshape(c.shape[:3] + (-1,)).astype(bf16)

    def front(c, rows):
        return jnp.pad(c, ((0, 0), (0, 0), (rows - c.shape[2], 0), (0, 0)))

    past_kv = jnp.concatenate(
        [front(heads_flat(cache_a_k), BAND_ROWS), front(heads_flat(cache_a_v), BAND_ROWS),
         front(heads_flat(cache_b_k), BAND_ROWS), front(heads_flat(cache_b_v), BAND_ROWS)], axis=3)
    mkv_s = jnp.concatenate([heads_flat(cache_mem_k), heads_flat(cache_mem_v)], axis=3)

    xp = x_prompt.reshape(b_p * t_p, d)
    xs = x_sample.reshape(b_s * t_s, d)
    mem_flat = mem_prompt.reshape(b_p * n_mem, d)
    outs = [[] for _ in range(10)]
    for l in range(depth):
        mkv_f, mkv_b = _memkv(mem_flat, g_mem, w_mem_b, qk_gain, l)
        xp = _ffn(xp, g_ff1, w1gu, w1o, l)
        q, kvb, kvf = _proj(xp, g_mix, w_in_b, qk_gain, tab_p, t_p, l)
        o = _attention(q.reshape(b_p, t_p, Q_COLS), kvb.reshape(b_p, t_p, KV_COLS),
                       mkv_b.reshape(b_p, n_mem, MEM_COLS), attn_sinks, bias_a_p, bias_b_p,
                       PROMPT_Q_TILE, 0, l)
        xp = _outproj(xp, o.reshape(b_p * t_p, Q_COLS), w_out_b, l)
        xp = _ffn(xp, g_ff2, w2gu, w2o, l)
        kvf = kvf.reshape(b_p, t_p, KV_COLS)
        mkv_f = mkv_f.reshape(b_p, n_mem, MEM_COLS)
        outs[0].append(kvf[:, t_p - keep_a:, KA_OFF:VA_OFF].reshape(b_p, keep_a, H_A_KV, HEAD_DIM))
        outs[1].append(kvf[:, t_p - keep_a:, VA_OFF:KB_OFF].reshape(b_p, keep_a, H_A_KV, HEAD_DIM))
        outs[2].append(kvf[:, t_p - keep_b:, KB_OFF:VB_OFF].reshape(b_p, keep_b, H_B, HEAD_DIM))
        outs[3].append(kvf[:, t_p - keep_b:, VB_OFF:].reshape(b_p, keep_b, H_B, HEAD_DIM))
        outs[4].append(mkv_f[:, :, :H_M * HEAD_DIM].reshape(b_p, n_mem, H_M, HEAD_DIM))
        outs[5].append(mkv_f[:, :, H_M * HEAD_DIM:].reshape(b_p, n_mem, H_M, HEAD_DIM))

        xs = _ffn(xs, g_ff1, w1gu, w1o, l)
        q, kvb, kvf = _proj(xs, g_mix, w_in_b, qk_gain, tab_s, t_s, l)
        q = jnp.pad(q.reshape(b_s, t_s, Q_COLS), ((0, 0), (0, CHUNK - t_s), (0, 0)))
        kv = jnp.concatenate(
            [past_kv[l], kvb.reshape(b_s, t_s, KV_COLS), jnp.zeros((b_s, CHUNK - t_s, KV_COLS), bf16)],
            axis=1)
        o = _attention(q, kv, mkv_s[l], attn_sinks, bias_a_s, bias_b_s, CHUNK, BAND_ROWS, l)
        xs = _outproj(xs, o[:, :t_s].reshape(b_s * t_s, Q_COLS), w_out_b, l)
        xs = _ffn(xs, g_ff2, w2gu, w2o, l)
        kvf = kvf.reshape(b_s, t_s, KV_COLS)
        outs[6].append(kvf[:, :, KA_OFF:VA_OFF].reshape(b_s, t_s, H_A_KV, HEAD_DIM))
        outs[7].append(kvf[:, :, VA_OFF:KB_OFF].reshape(b_s, t_s, H_A_KV, HEAD_DIM))
        outs[8].append(kvf[:, :, KB_OFF:VB_OFF].reshape(b_s, t_s, H_B, HEAD_DIM))
        outs[9].append(kvf[:, :, VB_OFF:].reshape(b_s, t_s, H_B, HEAD_DIM))

    return (xp.reshape(b_p, t_p, d), xs.reshape(b_s, t_s, d)) + tuple(jnp.stack(o) for o in outs)
```

```python
import functools
import math

import numpy as np
import jax
import jax.numpy as jnp
from jax import lax
from jax.experimental import pallas as pl
from jax.experimental.pallas import tpu as pltpu

CHUNK = 64
HEAD_DIM = 128
H_A = 8
H_A_KV = 2
WIN_CHUNKS = 2
WINDOW = WIN_CHUNKS * CHUNK
H_B = 4
BAND_CHUNKS = 8
BAND_ROWS = BAND_CHUNKS * CHUNK
REL_CLIP = 128
H_M = 4
ROT_DIM = HEAD_DIM // 4
ROPE_THETA = 500000.0
EPS = 1e-6
PAST_LEN = 1024
LOG2E = math.log2(math.e)
Q_SCALE = HEAD_DIM ** -0.5 * LOG2E

Q_COLS = (H_A + H_B + H_M) * HEAD_DIM
KV_COLS = (2 * H_A_KV + 2 * H_B) * HEAD_DIM
MEM_COLS = 2 * H_M * HEAD_DIM
QB_OFF = H_A * HEAD_DIM
QM_OFF = (H_A + H_B) * HEAD_DIM
KA_OFF = 0
VA_OFF = H_A_KV * HEAD_DIM
KB_OFF = 2 * H_A_KV * HEAD_DIM
VB_OFF = KB_OFF + H_B * HEAD_DIM

V7X_VMEM_LIMIT_BYTES = 58 * 1024 * 1024
HEAD_GROUP = 4
FF_TILE = 512
FFN_ROW_TILE = 1024
V7X_LANES = 128
V7X_MXU_DIM = 256
CAST_ROW_TILE = 256
ROW_TILE = 512
PROMPT_Q_TILE = 256
MASK_VALUE = -1e30

_NT_DIMS = (((1,), (1,)), ((), ()))


def _row_tile(rows, preferred):
    tile = min(rows, preferred)
    while rows % tile:
        tile //= 2
    return tile


def _params(semantics):
    return pltpu.CompilerParams(dimension_semantics=semantics,
                                vmem_limit_bytes=V7X_VMEM_LIMIT_BYTES)


def _rms_rows(x, g):
    ms = jnp.mean(x * x, axis=-1, keepdims=True)
    return x * lax.rsqrt(ms + EPS) * g


def _layer_spec(shape, index_map, **kwargs):
    return pl.BlockSpec((None,) + tuple(shape), index_map, **kwargs)


def _ffn_kernel(x_ref, g_ref, wg_ref, wu_ref, wo_ref, o_ref, xn_ref, *, n_ff, last_cols):
    j = pl.program_id(1)

    def accumulate_onto(base_ref, cols):
        xn = xn_ref[...]
        gate = jnp.dot(xn, wg_ref[:, :cols], preferred_element_type=jnp.float32)
        up = jnp.dot(xn, wu_ref[:, :cols], preferred_element_type=jnp.float32)
        act = (gate * jax.nn.sigmoid(gate) * up * 0.5).astype(jnp.bfloat16)
        o_ref[...] = base_ref[...] + jnp.dot(act, wo_ref[:cols, :], preferred_element_type=jnp.float32)

    def normalise():
        xn_ref[...] = _rms_rows(x_ref[...], g_ref[...]).astype(xn_ref.dtype)

    @pl.when(j == 0)
    def _():
        normalise()
        accumulate_onto(x_ref, FF_TILE if n_ff > 1 else last_cols)

    if last_cols == FF_TILE:
        @pl.when(j != 0)
        def _():
            accumulate_onto(o_ref, FF_TILE)
    elif n_ff > 1:
        @pl.when((j != 0) & (j != n_ff - 1))
        def _():
            accumulate_onto(o_ref, FF_TILE)

        @pl.when(j == n_ff - 1)
        def _():
            accumulate_onto(o_ref, last_cols)


def _ffn(x, g, wgu, wo, layer):
    rows, d = x.shape
    f = wo.shape[1]
    n_ff = pl.cdiv(f, FF_TILE)
    tm = _row_tile(rows, FFN_ROW_TILE)
    return pl.pallas_call(
        functools.partial(_ffn_kernel, n_ff=n_ff, last_cols=f - (n_ff - 1) * FF_TILE),
        grid=(rows // tm, n_ff),
        in_specs=[
            pl.BlockSpec((tm, d), lambda i, j: (i, 0)),
            _layer_spec((1, d), lambda i, j: (layer, 0, 0)),
            pl.BlockSpec((None, None, d, FF_TILE), lambda i, j: (0, layer, 0, j)),
            pl.BlockSpec((None, None, d, FF_TILE), lambda i, j: (1, layer, 0, j)),
            _layer_spec((FF_TILE, d), lambda i, j: (layer, j, 0)),
        ],
        out_specs=pl.BlockSpec((tm, d), lambda i, j: (i, 0)),
        out_shape=jax.ShapeDtypeStruct((rows, d), jnp.float32),
        scratch_shapes=[pltpu.VMEM((tm, d), jnp.bfloat16)],
        compiler_params=_params(("parallel", "arbitrary")),
        name="ffn",
    )(x, g, wgu, wgu, wo)


def _split_cast_kernel(w_ref, o_ref):
    o_ref[...] = w_ref[...].astype(o_ref.dtype)


def _split_cast(w_i):
    depth, d, two_f = w_i.shape
    f = two_f // 2
    assert f % V7X_LANES == 0
    rt = _row_tile(d, CAST_ROW_TILE)
    return pl.pallas_call(
        _split_cast_kernel,
        grid=(depth, 2, d // rt),
        in_specs=[_layer_spec((rt, f), lambda l, p, r: (l, r, p))],
        out_specs=pl.BlockSpec((None, None, rt, f), lambda l, p, r: (p, l, r, 0)),
        out_shape=jax.ShapeDtypeStruct((2, depth, d, f), jnp.bfloat16),
        compiler_params=_params(("parallel", "parallel", "parallel")),
        name="split_cast",
    )(w_i)


def _head_plan():
    plan = []
    for h in range(H_A):
        plan.append((0, True, True, h * HEAD_DIM))
    for h in range(H_A_KV):
        plan.append((1, True, False, KA_OFF + h * HEAD_DIM))
    for h in range(H_A_KV):
        plan.append((None, False, False, VA_OFF + h * HEAD_DIM))
    for h in range(H_B):
        plan.append((2, False, True, QB_OFF + h * HEAD_DIM))
    for h in range(H_B):
        plan.append((3, False, False, KB_OFF + h * HEAD_DIM))
    for h in range(H_B):
        plan.append((None, False, False, VB_OFF + h * HEAD_DIM))
    for h in range(H_M):
        plan.append((4, False, True, QM_OFF + h * HEAD_DIM))
    return plan


_PLAN = _head_plan()
IN_COLS = len(_PLAN) * HEAD_DIM


def _proj_kernel(x_ref, g_ref, w_ref, qkg_ref, cos_ref, sa_ref, sb_ref, q_ref, kvb_ref, kvf_ref):
    xn = _rms_rows(x_ref[...], g_ref[...]).astype(jnp.bfloat16)
    cos = cos_ref[...]
    sa = sa_ref[...]
    sb = sb_ref[...]
    gw = HEAD_GROUP * HEAD_DIM
    groups = sorted(range(len(_PLAN) // HEAD_GROUP),
                    key=lambda g: all(_PLAN[g * HEAD_GROUP + hh][0] is None for hh in range(HEAD_GROUP)))
    for grp in groups:
        p = jnp.dot(xn, w_ref[:, grp * gw:(grp + 1) * gw], preferred_element_type=jnp.float32)
        for hh in range(HEAD_GROUP):
            gain, rotary, is_query, col = _PLAN[grp * HEAD_GROUP + hh]
            z = p[:, hh * HEAD_DIM:(hh + 1) * HEAD_DIM]
            if gain is not None:
                z = _rms_rows(z, qkg_ref[gain:gain + 1, :])
            if rotary:
                z = (z * cos + pltpu.roll(z, HEAD_DIM - ROT_DIM // 2, 1) * sa
                     + pltpu.roll(z, ROT_DIM // 2, 1) * sb)
            cols = slice(col, col + HEAD_DIM)
            if is_query:
                q_ref[:, cols] = (z * Q_SCALE).astype(q_ref.dtype)
            else:
                kvf_ref[:, cols] = z
                kvb_ref[:, cols] = z.astype(kvb_ref.dtype)


def _rope_tables(pos):
    half = ROT_DIM // 2
    inv_freq = ROPE_THETA ** (-jnp.arange(half, dtype=jnp.float32) / half)
    ang = pos.astype(jnp.float32)[:, None] * inv_freq[None, :]
    cos = jnp.cos(ang)
    sin = jnp.sin(ang)
    t = pos.shape[0]
    ones = jnp.ones((t, HEAD_DIM - ROT_DIM), jnp.float32)
    cos_t = jnp.concatenate([cos, cos, ones], axis=1)
    sa_t = jnp.concatenate([-sin, jnp.zeros((t, HEAD_DIM - half), jnp.float32)], axis=1)
    sb_t = jnp.concatenate([jnp.zeros((t, half), jnp.float32), sin,
                            jnp.zeros((t, HEAD_DIM - ROT_DIM), jnp.float32)], axis=1)
    return cos_t, sa_t, sb_t


def _proj(x, g, w_in, qk_gain, tables, seq_len, layer):
    rows, d = x.shape
    tm = _row_tile(rows, ROW_TILE)
    if tm <= seq_len:
        assert seq_len % tm == 0
        per_seq = seq_len // tm
        tab_map = lambda i: (i % per_seq, 0)
    else:
        assert tm % seq_len == 0
        tables = tuple(jnp.tile(t, (tm // seq_len, 1)) for t in tables)
        tab_map = lambda i: (0, 0)
    tab_spec = pl.BlockSpec((tm, HEAD_DIM), tab_map)
    return pl.pallas_call(
        _proj_kernel,
        grid=(rows // tm,),
        in_specs=[
            pl.BlockSpec((tm, d), lambda i: (i, 0)),
            _layer_spec((1, d), lambda i: (layer, 0, 0)),
            _layer_spec((d, IN_COLS), lambda i: (layer, 0, 0)),
            _layer_spec(qk_gain.shape[1:], lambda i: (layer, 0, 0)),
            tab_spec, tab_spec, tab_spec,
        ],
        out_specs=[
            pl.BlockSpec((tm, Q_COLS), lambda i: (i, 0)),
            pl.BlockSpec((tm, KV_COLS), lambda i: (i, 0)),
            pl.BlockSpec((tm, KV_COLS), lambda i: (i, 0)),
        ],
        out_shape=[
            jax.ShapeDtypeStruct((rows, Q_COLS), jnp.bfloat16),
            jax.ShapeDtypeStruct((rows, KV_COLS), jnp.bfloat16),
            jax.ShapeDtypeStruct((rows, KV_COLS), jnp.float32),
        ],
        compiler_params=_params(("parallel",)),
        name="proj",
    )(x, g, w_in, qk_gain, *tables)


def _memkv_kernel(x_ref, g_ref, w_ref, qkg_ref, of_ref, ob_ref):
    xn = _rms_rows(x_ref[...], g_ref[...]).astype(jnp.bfloat16)
    gw = H_M * HEAD_DIM
    for part in range(2):
        p = jnp.dot(xn, w_ref[:, part * gw:(part + 1) * gw], preferred_element_type=jnp.float32)
        for h in range(H_M):
            z = p[:, h * HEAD_DIM:(h + 1) * HEAD_DIM]
            if part == 0:
                z = _rms_rows(z, qkg_ref[5:6, :])
            cols = slice(part * gw + h * HEAD_DIM, part * gw + (h + 1) * HEAD_DIM)
            of_ref[:, cols] = z
            ob_ref[:, cols] = z.astype(ob_ref.dtype)


def _memkv(mem, g, w_kv, qk_gain, layer):
    rows, d = mem.shape
    tm = _row_tile(rows, ROW_TILE)
    return pl.pallas_call(
        _memkv_kernel,
        grid=(rows // tm,),
        in_specs=[
            pl.BlockSpec((tm, d), lambda i: (i, 0)),
            _layer_spec((1, d), lambda i: (layer, 0, 0)),
            _layer_spec((d, MEM_COLS), lambda i: (layer, 0, 0)),
            _layer_spec(qk_gain.shape[1:], lambda i: (layer, 0, 0)),
        ],
        out_specs=[pl.BlockSpec((tm, MEM_COLS), lambda i: (i, 0))] * 2,
        out_shape=[jax.ShapeDtypeStruct((rows, MEM_COLS), jnp.float32),
                   jax.ShapeDtypeStruct((rows, MEM_COLS), jnp.bfloat16)],
        compiler_params=_params(("parallel",)),
        name="memkv",
    )(mem, g, w_kv, qk_gain)


def _attn_kernel(sink_ref, q_ref, kv_ref, mkv_ref, ba_ref, bb_ref, x_ref, w_ref, y_ref, o_ref, *,
                 layer, q_tile, base, sub_a, pieces_b):
    tile = pl.program_id(1)
    row0 = pl.multiple_of(tile * q_tile, q_tile)
    n_sub = q_tile // sub_a
    n_var_a = ba_ref.shape[0]
    group = H_A // H_A_KV

    def piece_rows(off, size):
        start = base + row0 + off
        if base + off < 0:
            start = jnp.maximum(start, 0)
        return pl.ds(pl.multiple_of(start, math.gcd(q_tile, abs(off), base)), size)

    def window_a(u):
        start = base + row0 + u * sub_a - WINDOW
        if base + u * sub_a - WINDOW < 0:
            start = jnp.maximum(start, 0)
        return pl.ds(pl.multiple_of(start, math.gcd(sub_a, base)), WINDOW + sub_a)

    def bias_a(u):
        return ba_ref[jnp.minimum(tile * n_sub + u, n_var_a - 1)]

    jobs = []
    for g in range(H_A_KV):
        heads = range(g * group, (g + 1) * group)
        sinks = [sink_ref[layer, h] * LOG2E for h in heads]
        for u in range(n_sub):
            jobs.append((slice(u * sub_a, (u + 1) * sub_a), [h * HEAD_DIM for h in heads], kv_ref,
                         [window_a(u)], KA_OFF + g * HEAD_DIM, VA_OFF + g * HEAD_DIM,
                         functools.partial(bias_a, u), sinks))
    rows_b = [piece_rows(off, size) for off, size in pieces_b]
    for h in range(H_B):
        jobs.append((slice(0, q_tile), [QB_OFF + h * HEAD_DIM], kv_ref, rows_b, KB_OFF + h * HEAD_DIM,
                     VB_OFF + h * HEAD_DIM, functools.partial(lambda hh: bb_ref[0, hh], h), None))
    rows_m = [pl.ds(0, mkv_ref.shape[1])]
    for h in range(H_M):
        jobs.append((slice(0, q_tile), [QM_OFF + h * HEAD_DIM], mkv_ref, rows_m, h * HEAD_DIM,
                     (H_M + h) * HEAD_DIM, None, None))

    scores = []
    for q_rows, q_cols, k_ref, rows_list, k_col, _, _, _ in jobs:
        qs = [q_ref[0, q_rows, c:c + HEAD_DIM] for c in q_cols]
        q = qs[0] if len(qs) == 1 else jnp.concatenate(qs, axis=0)
        parts = [lax.dot_general(q, k_ref[0, rows, k_col:k_col + HEAD_DIM], _NT_DIMS,
                                 preferred_element_type=jnp.float32) for rows in rows_list]
        scores.append(parts[0] if len(parts) == 1 else jnp.concatenate(parts, axis=1))

    probs = []
    dens = []
    for (q_rows, q_cols, _, _, _, _, bias, sinks), s_all in zip(jobs, scores):
        n_rows = q_rows.stop - q_rows.start
        bias_val = None if bias is None else bias()
        es = []
        ds = []
        for i in range(len(q_cols)):
            s = s_all[i * n_rows:(i + 1) * n_rows]
            if bias_val is not None:
                s = s + bias_val
            m = jnp.max(s, axis=-1, keepdims=True)
            if sinks is not None:
                m = jnp.maximum(m, sinks[i])
            e = jnp.exp2(s - m)
            den = jnp.sum(e, axis=-1, keepdims=True)
            if sinks is not None:
                den = den + jnp.exp2(sinks[i] - m)
            es.append(e.astype(jnp.bfloat16))
            ds.append(den)
        probs.append(es[0] if len(es) == 1 else jnp.concatenate(es, axis=0))
        dens.append(ds)

    for (q_rows, q_cols, k_ref, rows_list, _, v_col, _, _), e, ds in zip(jobs, probs, dens):
        n_rows = q_rows.stop - q_rows.start
        acc = None
        lo = 0
        for rows in rows_list:
            pv = jnp.dot(e[:, lo:lo + rows.size], k_ref[0, rows, v_col:v_col + HEAD_DIM],
                         preferred_element_type=jnp.float32)
            acc = pv if acc is None else acc + pv
            lo += rows.size
        for i, c in enumerate(q_cols):
            o = acc[i * n_rows:(i + 1) * n_rows] / ds[i]
            o_ref[q_rows, c:c + HEAD_DIM] = o.astype(o_ref.dtype)

    y = x_ref[0]
    for c in range(0, Q_COLS, V7X_MXU_DIM):
        y = y + jnp.dot(o_ref[:, c:c + V7X_MXU_DIM], w_ref[c:c + V7X_MXU_DIM, :],
                        preferred_element_type=jnp.float32)
    y_ref[0] = y


def _window_pieces(past_rows, q_tile, base):
    size = past_rows if base >= past_rows else math.gcd(past_rows, q_tile)
    pieces = [(-past_rows + k * size, size) for k in range(past_rows // size)]
    return pieces + [(0, q_tile)]


def _leading_variants(masks):
    n_var = len(masks)
    while n_var > 1 and np.array_equal(masks[n_var - 2], masks[-1]):
        n_var -= 1
    return np.stack(masks[:n_var])


def _band_masks(q_tile, past_rows, n_past_chunks, n_tiles, first_real, new_rows):
    win = past_rows + q_tile
    r = np.arange(q_tile)[:, None] // CHUNK
    w = np.arange(win)[None, :]
    band = (w // CHUNK >= r) & (w // CHUNK <= r + n_past_chunks)
    masks = []
    for i in range(n_tiles):
        row = i * q_tile + w - past_rows
        masks.append(band & (row >= first_real) & (row < new_rows))
    return _leading_variants(masks)


def _window_masks_a(sub, n_sub_tiles, base, first_real, new_rows):
    masks = []
    for s in range(n_sub_tiles):
        q_row = s * sub + np.arange(sub)[:, None]
        k_row = max(s * sub - WINDOW, -base) + np.arange(WINDOW + sub)[None, :]
        q_chunk = q_row // CHUNK
        k_chunk = np.floor_divide(k_row, CHUNK)
        masks.append((k_chunk >= q_chunk - WIN_CHUNKS) & (k_chunk <= q_chunk)
                     & (k_row >= first_real) & (k_row < new_rows))
    return _leading_variants(masks)


def _rel_bias_tiles(rel_tab, q_tile):
    win = BAND_ROWS + q_tile
    n = win + q_tile
    k = np.arange(n)
    dist = np.where(k < win, BAND_ROWS - k, BAND_ROWS + n - k)
    seq = rel_tab[:, :, np.clip(dist, -REL_CLIP, REL_CLIP) + REL_CLIP].astype(jnp.float32) * LOG2E
    flat = jnp.tile(seq, (1, 1, q_tile))[:, :, :q_tile * (n - 1)]
    return flat.reshape(rel_tab.shape[0], H_B, q_tile, n - 1)[..., :win]


def _attn_biases(rel_tab, q_tile, n_tiles, base, past_a, past_b, new_rows):
    sub = min(q_tile, WINDOW)
    mask_a = _window_masks_a(sub, n_tiles * (q_tile // sub), base, -past_a, new_rows)
    mask_b = _band_masks(q_tile, BAND_ROWS, BAND_CHUNKS, n_tiles, -past_b, new_rows)
    bias_a = jnp.asarray(np.where(mask_a, 0.0, MASK_VALUE), jnp.float32)
    rel = _rel_bias_tiles(rel_tab, q_tile)
    bias_b = jnp.where(jnp.asarray(mask_b)[None, :, None], rel[:, None], MASK_VALUE)
    return bias_a, bias_b


def _mixer(x, q, kv, mkv, sinks, bias_a, bias_b, w_out, q_tile, base, layer):
    b, tq, d = x.shape
    assert kv.shape[1] == base + tq and q.shape[1] == tq
    n_b = bias_b.shape[1]
    win_b = BAND_ROWS + q_tile
    kern = functools.partial(_attn_kernel, layer=layer, q_tile=q_tile, base=base,
                             sub_a=bias_a.shape[1],
                             pieces_b=_window_pieces(BAND_ROWS, q_tile, base))
    return pl.pallas_call(
        kern,
        grid=(b, tq // q_tile),
        in_specs=[
            pl.BlockSpec(memory_space=pltpu.SMEM),
            pl.BlockSpec((1, q_tile, Q_COLS), lambda bi, i: (bi, i, 0)),
            pl.BlockSpec((1,) + kv.shape[1:], lambda bi, i: (bi, 0, 0)),
            pl.BlockSpec((1,) + mkv.shape[1:], lambda bi, i: (bi, 0, 0)),
            pl.BlockSpec(bias_a.shape, lambda bi, i: (0, 0, 0)),
            _layer_spec((1, H_B, q_tile, win_b),
                        lambda bi, i: (layer, jnp.minimum(i, n_b - 1), 0, 0, 0)),
            pl.BlockSpec((1, q_tile, d), lambda bi, i: (bi, i, 0)),
            _layer_spec((Q_COLS, d), lambda bi, i: (layer, 0, 0), pipeline_mode=pl.Buffered(1)),
        ],
        out_specs=pl.BlockSpec((1, q_tile, d), lambda bi, i: (bi, i, 0)),
        out_shape=jax.ShapeDtypeStruct((b, tq, d), jnp.float32),
        scratch_shapes=[pltpu.VMEM((q_tile, Q_COLS), jnp.bfloat16)],
        compiler_params=_params(("parallel", "arbitrary")),
        name="mixer",
    )(sinks, q, kv, mkv, bias_a, bias_b, x, w_out)


def kernel(x_prompt, x_sample, cache_a_k, cache_a_v, cache_b_k, cache_b_v, cache_mem_k, cache_mem_v,
           mem_prompt, norm_ff1, w_ff1_in, w_ff1_out, norm_mix, w_in, qk_gain, attn_sinks, rel_bias,
           norm_mem, w_mem_kv, w_out, norm_ff2, w_ff2_in, w_ff2_out):
    b_p, t_p, d = x_prompt.shape
    b_s, t_s, _ = x_sample.shape
    depth = w_in.shape[0]
    n_mem = mem_prompt.shape[1]
    bf16 = jnp.bfloat16
    assert t_p % PROMPT_Q_TILE == 0 and t_s <= CHUNK

    w1gu = _split_cast(w_ff1_in)
    w2gu = _split_cast(w_ff2_in)
    w1o = w_ff1_out.astype(bf16)
    w2o = w_ff2_out.astype(bf16)
    w_in_b = w_in.astype(bf16)
    w_out_b = w_out.astype(bf16)
    w_mem_b = w_mem_kv.astype(bf16)
    g_ff1 = norm_ff1[:, None, :]
    g_ff2 = norm_ff2[:, None, :]
    g_mix = norm_mix[:, None, :]
    g_mem = norm_mem[:, None, :]

    tab_p = _rope_tables(jnp.arange(t_p, dtype=jnp.int32))
    tab_s = _rope_tables(PAST_LEN + jnp.arange(t_s, dtype=jnp.int32))
    keep_a = min(WINDOW, t_p)
    keep_b = min(BAND_ROWS, t_p)
    la = cache_a_k.shape[2]
    lb = cache_b_k.shape[2]

    bias_a_p, bias_b_p = _attn_biases(rel_bias, PROMPT_Q_TILE, t_p // PROMPT_Q_TILE, 0, 0, 0, t_p)
    bias_a_s, bias_b_s = _attn_biases(rel_bias, CHUNK, 1, BAND_ROWS, la, lb, t_s)

    def heads_flat(c):
        return c.reshape(c.shape[:3] + (-1,)).astype(bf16)

    def front(c, rows):
        return jnp.pad(c, ((0, 0), (0, 0), (rows - c.shape[2], 0), (0, 0)))

    past_kv = jnp.concatenate(
        [front(heads_flat(cache_a_k), BAND_ROWS), front(heads_flat(cache_a_v), BAND_ROWS),
         front(heads_flat(cache_b_k), BAND_ROWS), front(heads_flat(cache_b_v), BAND_ROWS)], axis=3)
    mkv_s = jnp.concatenate([heads_flat(cache_mem_k), heads_flat(cache_mem_v)], axis=3)

    xp = x_prompt.reshape(b_p * t_p, d)
    xs = x_sample.reshape(b_s * t_s, d)
    mem_flat = mem_prompt.reshape(b_p * n_mem, d)
    outs = [[] for _ in range(10)]
    for l in range(depth):
        mkv_f, mkv_b = _memkv(mem_flat, g_mem, w_mem_b, qk_gain, l)
        xp = _ffn(xp, g_ff1, w1gu, w1o, l)
        q, kvb, kvf = _proj(xp, g_mix, w_in_b, qk_gain, tab_p, t_p, l)
        xp = _mixer(xp.reshape(b_p, t_p, d), q.reshape(b_p, t_p, Q_COLS),
                    kvb.reshape(b_p, t_p, KV_COLS), mkv_b.reshape(b_p, n_mem, MEM_COLS),
                    attn_sinks, bias_a_p, bias_b_p, w_out_b, PROMPT_Q_TILE, 0, l).reshape(b_p * t_p, d)
        xp = _ffn(xp, g_ff2, w2gu, w2o, l)
        kvf = kvf.reshape(b_p, t_p, KV_COLS)
        outs[0].append(kvf[:, t_p - keep_a:, KA_OFF:VA_OFF])
        outs[1].append(kvf[:, t_p - keep_a:, VA_OFF:KB_OFF])
        outs[2].append(kvf[:, t_p - keep_b:, KB_OFF:VB_OFF])
        outs[3].append(kvf[:, t_p - keep_b:, VB_OFF:])
        outs[4].append(mkv_f[:, :H_M * HEAD_DIM])
        outs[5].append(mkv_f[:, H_M * HEAD_DIM:])

        xs = _ffn(xs, g_ff1, w1gu, w1o, l)
        q, kvb, kvf = _proj(xs, g_mix, w_in_b, qk_gain, tab_s, t_s, l)
        tail = ((0, 0), (0, CHUNK - t_s), (0, 0))
        kv = jnp.concatenate([past_kv[l], jnp.pad(kvb.reshape(b_s, t_s, KV_COLS), tail)], axis=1)
        y = _mixer(jnp.pad(xs.reshape(b_s, t_s, d), tail), jnp.pad(q.reshape(b_s, t_s, Q_COLS), tail),
                   kv, mkv_s[l], attn_sinks, bias_a_s, bias_b_s, w_out_b, CHUNK, BAND_ROWS, l)
        xs = _ffn(y[:, :t_s].reshape(b_s * t_s, d), g_ff2, w2gu, w2o, l)
        outs[6].append(kvf[:, KA_OFF:VA_OFF])
        outs[7].append(kvf[:, VA_OFF:KB_OFF])
        outs[8].append(kvf[:, KB_OFF:VB_OFF])
        outs[9].append(kvf[:, VB_OFF:])

    def stacked_heads(per_layer, batch):
        z = jnp.stack(per_layer)
        return z.reshape(depth, batch, -1, z.shape[-1] // HEAD_DIM, HEAD_DIM)

    return ((xp.reshape(b_p, t_p, d), xs.reshape(b_s, t_s, d))
            + tuple(stacked_heads(o, b_p) for o in outs[:6])
            + tuple(stacked_heads(o, b_s) for o in outs[6:]))
```

```python
import functools
import math

import numpy as np
import jax
import jax.numpy as jnp
from jax import lax
from jax.experimental import pallas as pl
from jax.experimental.pallas import tpu as pltpu

CHUNK = 64
HEAD_DIM = 128
H_A = 8
H_A_KV = 2
WIN_CHUNKS = 2
WINDOW = WIN_CHUNKS * CHUNK
H_B = 4
BAND_CHUNKS = 8
BAND_ROWS = BAND_CHUNKS * CHUNK
REL_CLIP = 128
H_M = 4
ROT_DIM = HEAD_DIM // 4
ROPE_THETA = 500000.0
EPS = 1e-6
PAST_LEN = 1024
LOG2E = math.log2(math.e)
Q_SCALE = HEAD_DIM ** -0.5 * LOG2E

Q_COLS = (H_A + H_B + H_M) * HEAD_DIM
KV_COLS = (2 * H_A_KV + 2 * H_B) * HEAD_DIM
MEM_COLS = 2 * H_M * HEAD_DIM
QB_OFF = H_A * HEAD_DIM
QM_OFF = (H_A + H_B) * HEAD_DIM
KA_OFF = 0
VA_OFF = H_A_KV * HEAD_DIM
KB_OFF = 2 * H_A_KV * HEAD_DIM
VB_OFF = KB_OFF + H_B * HEAD_DIM

V7X_VMEM_LIMIT_BYTES = 62 * 1024 * 1024
HEAD_GROUP = 4
FF_TILE = 768
FFN_ROW_TILE = 1024
V7X_LANES = 128
V7X_BF16_SUBLANES = 16
V7X_MXU_DIM = 256
CAST_ROW_TILE = 256
ROW_TILE = 512
PROMPT_Q_TILE = 256
MASK_VALUE = -1e30

_NT_DIMS = (((1,), (1,)), ((), ()))


def _row_tile(rows, preferred):
    tile = min(rows, preferred)
    while rows % tile:
        tile //= 2
    return tile


def _params(semantics):
    return pltpu.CompilerParams(dimension_semantics=semantics,
                                vmem_limit_bytes=V7X_VMEM_LIMIT_BYTES)


def _rms_rows(x, g):
    ms = jnp.mean(x * x, axis=-1, keepdims=True)
    return x * lax.rsqrt(ms + EPS) * g


def _layer_spec(shape, index_map, **kwargs):
    return pl.BlockSpec((None,) + tuple(shape), index_map, **kwargs)


def _ffn_kernel(x_ref, g_ref, wg_ref, wu_ref, wo_ref, o_ref, xn_ref, *, n_ff, last_cols):
    j = pl.program_id(1)

    def accumulate_onto(base_ref, cols):
        xn = xn_ref[...]
        gate = jnp.dot(xn, wg_ref[:, :cols], preferred_element_type=jnp.float32)
        up = jnp.dot(xn, wu_ref[:, :cols], preferred_element_type=jnp.float32)
        act = (gate * jax.nn.sigmoid(gate) * up * 0.5).astype(jnp.bfloat16)
        o_ref[...] = base_ref[...] + jnp.dot(act, wo_ref[:cols, :], preferred_element_type=jnp.float32)

    def normalise():
        xn_ref[...] = _rms_rows(x_ref[...], g_ref[...]).astype(xn_ref.dtype)

    @pl.when(j == 0)
    def _():
        normalise()
        accumulate_onto(x_ref, FF_TILE if n_ff > 1 else last_cols)

    if last_cols == FF_TILE:
        @pl.when(j != 0)
        def _():
            accumulate_onto(o_ref, FF_TILE)
    elif n_ff > 1:
        @pl.when((j != 0) & (j != n_ff - 1))
        def _():
            accumulate_onto(o_ref, FF_TILE)

        @pl.when(j == n_ff - 1)
        def _():
            accumulate_onto(o_ref, last_cols)


def _ffn(x, g, weights, layer):
    wg, wu, wo = weights
    rows, d = x.shape
    f = wo.shape[0]
    n_ff = pl.cdiv(f, FF_TILE)
    tm = _row_tile(rows, FFN_ROW_TILE)
    return pl.pallas_call(
        functools.partial(_ffn_kernel, n_ff=n_ff, last_cols=f - (n_ff - 1) * FF_TILE),
        grid=(rows // tm, n_ff),
        in_specs=[
            pl.BlockSpec((tm, d), lambda i, j: (i, 0)),
            _layer_spec((1, d), lambda i, j: (layer, 0, 0)),
            pl.BlockSpec((d, FF_TILE), lambda i, j: (0, j)),
            pl.BlockSpec((d, FF_TILE), lambda i, j: (0, j)),
            pl.BlockSpec((FF_TILE, d), lambda i, j: (j, 0)),
        ],
        out_specs=pl.BlockSpec((tm, d), lambda i, j: (i, 0)),
        out_shape=jax.ShapeDtypeStruct((rows, d), jnp.float32),
        scratch_shapes=[pltpu.VMEM((tm, d), jnp.bfloat16)],
        compiler_params=_params(("parallel", "arbitrary")),
        name="ffn",
    )(x, g, wg, wu, wo)


def _cast_kernel(w_ref, o_ref):
    o_ref[...] = w_ref[...].astype(o_ref.dtype)


def _ffn_weights_bf16(w_i, w_o, layer):
    _, d, two_f = w_i.shape
    f = two_f // 2
    assert f % V7X_LANES == 0
    rt = _row_tile(d, CAST_ROW_TILE)

    def half(part):
        return pl.pallas_call(
            _cast_kernel,
            grid=(d // rt,),
            in_specs=[_layer_spec((rt, f), lambda r: (layer, r, part))],
            out_specs=pl.BlockSpec((rt, f), lambda r: (r, 0)),
            out_shape=jax.ShapeDtypeStruct((d, f), jnp.bfloat16),
            compiler_params=_params(("parallel",)),
            name="cast_cols",
        )(w_i)

    return half(0), half(1), w_o[layer].astype(jnp.bfloat16)


def _head_plan():
    plan = []
    for h in range(H_A):
        plan.append((0, True, True, h * HEAD_DIM))
    for h in range(H_A_KV):
        plan.append((1, True, False, KA_OFF + h * HEAD_DIM))
    for h in range(H_A_KV):
        plan.append((None, False, False, VA_OFF + h * HEAD_DIM))
    for h in range(H_B):
        plan.append((2, False, True, QB_OFF + h * HEAD_DIM))
    for h in range(H_B):
        plan.append((3, False, False, KB_OFF + h * HEAD_DIM))
    for h in range(H_B):
        plan.append((None, False, False, VB_OFF + h * HEAD_DIM))
    for h in range(H_M):
        plan.append((4, False, True, QM_OFF + h * HEAD_DIM))
    return plan


_PLAN = _head_plan()
IN_COLS = len(_PLAN) * HEAD_DIM


def _proj_kernel(x_ref, g_ref, w_ref, qkg_ref, cos_ref, sa_ref, sb_ref, *rest, n_cast):
    cast_in = rest[:n_cast]
    q_ref, kvb_ref, kvf_ref = rest[n_cast:n_cast + 3]
    cast_out = rest[n_cast + 3:]
    for src, dst in zip(cast_in, cast_out):
        dst[...] = src[...].astype(dst.dtype)

    xn = _rms_rows(x_ref[...], g_ref[...]).astype(jnp.bfloat16)
    cos = cos_ref[...]
    sa = sa_ref[...]
    sb = sb_ref[...]
    gw = HEAD_GROUP * HEAD_DIM
    groups = sorted(range(len(_PLAN) // HEAD_GROUP),
                    key=lambda g: all(_PLAN[g * HEAD_GROUP + hh][0] is None for hh in range(HEAD_GROUP)))
    for grp in groups:
        p = jnp.dot(xn, w_ref[:, grp * gw:(grp + 1) * gw], preferred_element_type=jnp.float32)
        for hh in range(HEAD_GROUP):
            gain, rotary, is_query, col = _PLAN[grp * HEAD_GROUP + hh]
            z = p[:, hh * HEAD_DIM:(hh + 1) * HEAD_DIM]
            if gain is not None:
                z = _rms_rows(z, qkg_ref[gain:gain + 1, :])
            if rotary:
                z = (z * cos + pltpu.roll(z, HEAD_DIM - ROT_DIM // 2, 1) * sa
                     + pltpu.roll(z, ROT_DIM // 2, 1) * sb)
            cols = slice(col, col + HEAD_DIM)
            if is_query:
                q_ref[:, cols] = (z * Q_SCALE).astype(q_ref.dtype)
            else:
                kvf_ref[:, cols] = z
                kvb_ref[:, cols] = z.astype(kvb_ref.dtype)


def _rope_tables(pos):
    half = ROT_DIM // 2
    inv_freq = ROPE_THETA ** (-jnp.arange(half, dtype=jnp.float32) / half)
    ang = pos.astype(jnp.float32)[:, None] * inv_freq[None, :]
    cos = jnp.cos(ang)
    sin = jnp.sin(ang)
    t = pos.shape[0]
    ones = jnp.ones((t, HEAD_DIM - ROT_DIM), jnp.float32)
    cos_t = jnp.concatenate([cos, cos, ones], axis=1)
    sa_t = jnp.concatenate([-sin, jnp.zeros((t, HEAD_DIM - half), jnp.float32)], axis=1)
    sb_t = jnp.concatenate([jnp.zeros((t, half), jnp.float32), sin,
                            jnp.zeros((t, HEAD_DIM - ROT_DIM), jnp.float32)], axis=1)
    return cos_t, sa_t, sb_t


def _ffn_cast_fits(rows, d, f):
    n_steps = rows // _row_tile(rows, ROW_TILE)
    return (d % n_steps == 0 and (d // n_steps) % V7X_BF16_SUBLANES == 0
            and f % V7X_LANES == 0 and n_steps >= f // V7X_LANES)


def _ffn_cast_streams(w_i, w_o, layer, n_steps):
    _, d, two_f = w_i.shape
    f = two_f // 2
    rd = d // n_steps
    last = f // V7X_LANES - 1
    in_specs = [
        _layer_spec((rd, f), lambda i: (layer, i, 0)),
        _layer_spec((rd, f), lambda i: (layer, i, 1)),
        _layer_spec((V7X_LANES, d), lambda i: (layer, jnp.minimum(i, last), 0)),
    ]
    out_specs = [
        pl.BlockSpec((rd, f), lambda i: (i, 0)),
        pl.BlockSpec((rd, f), lambda i: (i, 0)),
        pl.BlockSpec((V7X_LANES, d), lambda i: (jnp.minimum(i, last), 0)),
    ]
    out_shapes = [jax.ShapeDtypeStruct((d, f), jnp.bfloat16)] * 2 + [
        jax.ShapeDtypeStruct((f, d), jnp.bfloat16)]
    return [w_i, w_i, w_o], in_specs, out_specs, out_shapes


def _proj(x, g, w_in, qk_gain, tables, seq_len, layer, casts=()):
    rows, d = x.shape
    tm = _row_tile(rows, ROW_TILE)
    cast_args, cast_in, cast_out, cast_shapes = [], [], [], []
    for w_i, w_o, cast_layer in casts:
        a, i_s, o_s, shp = _ffn_cast_streams(w_i, w_o, cast_layer, rows // tm)
        cast_args += a
        cast_in += i_s
        cast_out += o_s
        cast_shapes += shp
    if tm <= seq_len:
        assert seq_len % tm == 0
        per_seq = seq_len // tm
        tab_map = lambda i: (i % per_seq, 0)
    else:
        assert tm % seq_len == 0
        tables = tuple(jnp.tile(t, (tm // seq_len, 1)) for t in tables)
        tab_map = lambda i: (0, 0)
    tab_spec = pl.BlockSpec((tm, HEAD_DIM), tab_map)
    return pl.pallas_call(
        functools.partial(_proj_kernel, n_cast=len(cast_args)),
        grid=(rows // tm,),
        in_specs=[
            pl.BlockSpec((tm, d), lambda i: (i, 0)),
            _layer_spec((1, d), lambda i: (layer, 0, 0)),
            _layer_spec((d, IN_COLS), lambda i: (layer, 0, 0), pipeline_mode=pl.Buffered(1)),
            _layer_spec(qk_gain.shape[1:], lambda i: (layer, 0, 0)),
            tab_spec, tab_spec, tab_spec,
        ] + cast_in,
        out_specs=[
            pl.BlockSpec((tm, Q_COLS), lambda i: (i, 0)),
            pl.BlockSpec((tm, KV_COLS), lambda i: (i, 0)),
            pl.BlockSpec((tm, KV_COLS), lambda i: (i, 0)),
        ] + cast_out,
        out_shape=[
            jax.ShapeDtypeStruct((rows, Q_COLS), jnp.bfloat16),
            jax.ShapeDtypeStruct((rows, KV_COLS), jnp.bfloat16),
            jax.ShapeDtypeStruct((rows, KV_COLS), jnp.float32),
        ] + cast_shapes,
        compiler_params=_params(("arbitrary",)),
        name="proj",
    )(x, g, w_in, qk_gain, *tables, *cast_args)


def _memkv_kernel(x_ref, g_ref, w_ref, qkg_ref, of_ref, ob_ref):
    xn = _rms_rows(x_ref[...], g_ref[...]).astype(jnp.bfloat16)
    gw = H_M * HEAD_DIM
    for part in range(2):
        p = jnp.dot(xn, w_ref[:, part * gw:(part + 1) * gw], preferred_element_type=jnp.float32)
        for h in range(H_M):
            z = p[:, h * HEAD_DIM:(h + 1) * HEAD_DIM]
            if part == 0:
                z = _rms_rows(z, qkg_ref[5:6, :])
            cols = slice(part * gw + h * HEAD_DIM, part * gw + (h + 1) * HEAD_DIM)
            of_ref[:, cols] = z
            ob_ref[:, cols] = z.astype(ob_ref.dtype)


def _memkv(mem, g, w_kv, qk_gain, layer):
    rows, d = mem.shape
    tm = _row_tile(rows, ROW_TILE)
    return pl.pallas_call(
        _memkv_kernel,
        grid=(rows // tm,),
        in_specs=[
            pl.BlockSpec((tm, d), lambda i: (i, 0)),
            _layer_spec((1, d), lambda i: (layer, 0, 0)),
            _layer_spec((d, MEM_COLS), lambda i: (layer, 0, 0)),
            _layer_spec(qk_gain.shape[1:], lambda i: (layer, 0, 0)),
        ],
        out_specs=[pl.BlockSpec((tm, MEM_COLS), lambda i: (i, 0))] * 2,
        out_shape=[jax.ShapeDtypeStruct((rows, MEM_COLS), jnp.float32),
                   jax.ShapeDtypeStruct((rows, MEM_COLS), jnp.bfloat16)],
        compiler_params=_params(("parallel",)),
        name="memkv",
    )(mem, g, w_kv, qk_gain)


def _attn_kernel(sink_ref, q_ref, kv_ref, mkv_ref, ba_ref, bb_ref, x_ref, w_ref, y_ref, o_ref, *,
                 layer, q_tile, base, sub_a, pieces_b):
    tile = pl.program_id(1)
    row0 = pl.multiple_of(tile * q_tile, q_tile)
    n_sub = q_tile // sub_a
    n_var_a = ba_ref.shape[0]
    group = H_A // H_A_KV

    def piece_rows(off, size):
        start = base + row0 + off
        if base + off < 0:
            start = jnp.maximum(start, 0)
        return pl.ds(pl.multiple_of(start, math.gcd(q_tile, abs(off), base)), size)

    def window_a(u):
        start = base + row0 + u * sub_a - WINDOW
        if base + u * sub_a - WINDOW < 0:
            start = jnp.maximum(start, 0)
        return pl.ds(pl.multiple_of(start, math.gcd(sub_a, base)), WINDOW + sub_a)

    def bias_a(u):
        return ba_ref[jnp.minimum(tile * n_sub + u, n_var_a - 1)]

    jobs = []
    for g in range(H_A_KV):
        heads = range(g * group, (g + 1) * group)
        sinks = [sink_ref[layer, h] * LOG2E for h in heads]
        for u in range(n_sub):
            jobs.append((slice(u * sub_a, (u + 1) * sub_a), [h * HEAD_DIM for h in heads], kv_ref,
                         [window_a(u)], KA_OFF + g * HEAD_DIM, VA_OFF + g * HEAD_DIM,
                         functools.partial(bias_a, u), sinks))
    rows_b = [piece_rows(off, size) for off, size in pieces_b]
    for h in range(H_B):
        jobs.append((slice(0, q_tile), [QB_OFF + h * HEAD_DIM], kv_ref, rows_b, KB_OFF + h * HEAD_DIM,
                     VB_OFF + h * HEAD_DIM, functools.partial(lambda hh: bb_ref[0, hh], h), None))
    rows_m = [pl.ds(0, mkv_ref.shape[1])]
    for h in range(H_M):
        jobs.append((slice(0, q_tile), [QM_OFF + h * HEAD_DIM], mkv_ref, rows_m, h * HEAD_DIM,
                     (H_M + h) * HEAD_DIM, None, None))

    scores = []
    for q_rows, q_cols, k_ref, rows_list, k_col, _, _, _ in jobs:
        qs = [q_ref[0, q_rows, c:c + HEAD_DIM] for c in q_cols]
        q = qs[0] if len(qs) == 1 else jnp.concatenate(qs, axis=0)
        parts = [lax.dot_general(q, k_ref[0, rows, k_col:k_col + HEAD_DIM], _NT_DIMS,
                                 preferred_element_type=jnp.float32) for rows in rows_list]
        scores.append(parts[0] if len(parts) == 1 else jnp.concatenate(parts, axis=1))

    probs = []
    dens = []
    for (q_rows, q_cols, _, _, _, _, bias, sinks), s_all in zip(jobs, scores):
        n_rows = q_rows.stop - q_rows.start
        bias_val = None if bias is None else bias()
        es = []
        ds = []
        for i in range(len(q_cols)):
            s = s_all[i * n_rows:(i + 1) * n_rows]
            if bias_val is not None:
                s = s + bias_val
            m = jnp.max(s, axis=-1, keepdims=True)
            if sinks is not None:
                m = jnp.maximum(m, sinks[i])
            e = jnp.exp2(s - m)
            den = jnp.sum(e, axis=-1, keepdims=True)
            if sinks is not None:
                den = den + jnp.exp2(sinks[i] - m)
            es.append(e.astype(jnp.bfloat16))
            ds.append(den)
        probs.append(es[0] if len(es) == 1 else jnp.concatenate(es, axis=0))
        dens.append(ds)

    for (q_rows, q_cols, k_ref, rows_list, _, v_col, _, _), e, ds in zip(jobs, probs, dens):
        n_rows = q_rows.stop - q_rows.start
        acc = None
        lo = 0
        for rows in rows_list:
            pv = jnp.dot(e[:, lo:lo + rows.size], k_ref[0, rows, v_col:v_col + HEAD_DIM],
                         preferred_element_type=jnp.float32)
            acc = pv if acc is None else acc + pv
            lo += rows.size
        for i, c in enumerate(q_cols):
            o = acc[i * n_rows:(i + 1) * n_rows] / ds[i]
            o_ref[q_rows, c:c + HEAD_DIM] = o.astype(o_ref.dtype)

    y = x_ref[0]
    for c in range(0, Q_COLS, V7X_MXU_DIM):
        y = y + jnp.dot(o_ref[:, c:c + V7X_MXU_DIM], w_ref[c:c + V7X_MXU_DIM, :],
                        preferred_element_type=jnp.float32)
    y_ref[0] = y


def _window_pieces(past_rows, q_tile, base):
    size = past_rows if base >= past_rows else math.gcd(past_rows, q_tile)
    pieces = [(-past_rows + k * size, size) for k in range(past_rows // size)]
    return pieces + [(0, q_tile)]


def _leading_variants(masks):
    n_var = len(masks)
    while n_var > 1 and np.array_equal(masks[n_var - 2], masks[-1]):
        n_var -= 1
    return np.stack(masks[:n_var])


def _band_masks(q_tile, past_rows, n_past_chunks, n_tiles, first_real, new_rows):
    win = past_rows + q_tile
    r = np.arange(q_tile)[:, None] // CHUNK
    w = np.arange(win)[None, :]
    band = (w // CHUNK >= r) & (w // CHUNK <= r + n_past_chunks)
    masks = []
    for i in range(n_tiles):
        row = i * q_tile + w - past_rows
        masks.append(band & (row >= first_real) & (row < new_rows))
    return _leading_variants(masks)


def _window_masks_a(sub, n_sub_tiles, base, first_real, new_rows):
    masks = []
    for s in range(n_sub_tiles):
        q_row = s * sub + np.arange(sub)[:, None]
        k_row = max(s * sub - WINDOW, -base) + np.arange(WINDOW + sub)[None, :]
        q_chunk = q_row // CHUNK
        k_chunk = np.floor_divide(k_row, CHUNK)
        masks.append((k_chunk >= q_chunk - WIN_CHUNKS) & (k_chunk <= q_chunk)
                     & (k_row >= first_real) & (k_row < new_rows))
    return _leading_variants(masks)


def _rel_bias_tiles(rel_tab, q_tile):
    win = BAND_ROWS + q_tile
    n = win + q_tile
    k = np.arange(n)
    dist = np.where(k < win, BAND_ROWS - k, BAND_ROWS + n - k)
    seq = rel_tab[:, :, np.clip(dist, -REL_CLIP, REL_CLIP) + REL_CLIP].astype(jnp.float32) * LOG2E
    flat = jnp.tile(seq, (1, 1, q_tile))[:, :, :q_tile * (n - 1)]
    return flat.reshape(rel_tab.shape[0], H_B, q_tile, n - 1)[..., :win]


def _attn_biases(rel_tab, q_tile, n_tiles, base, past_a, past_b, new_rows):
    sub = min(q_tile, WINDOW)
    mask_a = _window_masks_a(sub, n_tiles * (q_tile // sub), base, -past_a, new_rows)
    mask_b = _band_masks(q_tile, BAND_ROWS, BAND_CHUNKS, n_tiles, -past_b, new_rows)
    bias_a = jnp.asarray(np.where(mask_a, 0.0, MASK_VALUE), jnp.float32)
    rel = _rel_bias_tiles(rel_tab, q_tile)
    bias_b = jnp.where(jnp.asarray(mask_b)[None, :, None], rel[:, None], MASK_VALUE)
    return bias_a, bias_b


def _mixer(x, q, kv, mkv, sinks, bias_a, bias_b, w_out, q_tile, base, layer):
    b, tq, d = x.shape
    assert kv.shape[1] == base + tq and q.shape[1] == tq
    n_b = bias_b.shape[1]
    win_b = BAND_ROWS + q_tile
    kern = functools.partial(_attn_kernel, layer=layer, q_tile=q_tile, base=base,
                             sub_a=bias_a.shape[1],
                             pieces_b=_window_pieces(BAND_ROWS, q_tile, base))
    return pl.pallas_call(
        kern,
        grid=(b, tq // q_tile),
        in_specs=[
            pl.BlockSpec(memory_space=pltpu.SMEM),
            pl.BlockSpec((1, q_tile, Q_COLS), lambda bi, i: (bi, i, 0)),
            pl.BlockSpec((1,) + kv.shape[1:], lambda bi, i: (bi, 0, 0)),
            pl.BlockSpec((1,) + mkv.shape[1:], lambda bi, i: (bi, 0, 0)),
            pl.BlockSpec(bias_a.shape, lambda bi, i: (0, 0, 0)),
            _layer_spec((1, H_B, q_tile, win_b),
                        lambda bi, i: (layer, jnp.minimum(i, n_b - 1), 0, 0, 0)),
            pl.BlockSpec((1, q_tile, d), lambda bi, i: (bi, i, 0)),
            _layer_spec((Q_COLS, d), lambda bi, i: (layer, 0, 0), pipeline_mode=pl.Buffered(1)),
        ],
        out_specs=pl.BlockSpec((1, q_tile, d), lambda bi, i: (bi, i, 0)),
        out_shape=jax.ShapeDtypeStruct((b, tq, d), jnp.float32),
        scratch_shapes=[pltpu.VMEM((q_tile, Q_COLS), jnp.bfloat16)],
        compiler_params=_params(("parallel", "arbitrary")),
        name="mixer",
    )(sinks, q, kv, mkv, bias_a, bias_b, x, w_out)


def kernel(x_prompt, x_sample, cache_a_k, cache_a_v, cache_b_k, cache_b_v, cache_mem_k, cache_mem_v,
           mem_prompt, norm_ff1, w_ff1_in, w_ff1_out, norm_mix, w_in, qk_gain, attn_sinks, rel_bias,
           norm_mem, w_mem_kv, w_out, norm_ff2, w_ff2_in, w_ff2_out):
    b_p, t_p, d = x_prompt.shape
    b_s, t_s, _ = x_sample.shape
    depth = w_in.shape[0]
    n_mem = mem_prompt.shape[1]
    bf16 = jnp.bfloat16
    assert t_p % PROMPT_Q_TILE == 0 and t_s <= CHUNK

    ride = _ffn_cast_fits(b_p * t_p, d, w_ff1_out.shape[1])
    ff1_w = _ffn_weights_bf16(w_ff1_in, w_ff1_out, 0)
    w_in_b = w_in.astype(bf16)
    w_out_b = w_out.astype(bf16)
    w_mem_b = w_mem_kv.astype(bf16)
    g_ff1 = norm_ff1[:, None, :]
    g_ff2 = norm_ff2[:, None, :]
    g_mix = norm_mix[:, None, :]
    g_mem = norm_mem[:, None, :]

    tab_p = _rope_tables(jnp.arange(t_p, dtype=jnp.int32))
    tab_s = _rope_tables(PAST_LEN + jnp.arange(t_s, dtype=jnp.int32))
    keep_a = min(WINDOW, t_p)
    keep_b = min(BAND_ROWS, t_p)
    la = cache_a_k.shape[2]
    lb = cache_b_k.shape[2]

    bias_a_p, bias_b_p = _attn_biases(rel_bias, PROMPT_Q_TILE, t_p // PROMPT_Q_TILE, 0, 0, 0, t_p)
    bias_a_s, bias_b_s = _attn_biases(rel_bias, CHUNK, 1, BAND_ROWS, la, lb, t_s)

    def heads_flat(c):
        return c.reshape(c.shape[:3] + (-1,)).astype(bf16)

    def front(c, rows):
        return jnp.pad(c, ((0, 0), (0, 0), (rows - c.shape[2], 0), (0, 0)))

    past_kv = jnp.concatenate(
        [front(heads_flat(cache_a_k), BAND_ROWS), front(heads_flat(cache_a_v), BAND_ROWS),
         front(heads_flat(cache_b_k), BAND_ROWS), front(heads_flat(cache_b_v), BAND_ROWS)], axis=3)
    mkv_s = jnp.concatenate([heads_flat(cache_mem_k), heads_flat(cache_mem_v)], axis=3)

    xp = x_prompt.reshape(b_p * t_p, d)
    xs = x_sample.reshape(b_s * t_s, d)
    mem_flat = mem_prompt.reshape(b_p * n_mem, d)
    outs = [[] for _ in range(10)]
    for l in range(depth):
        mkv_f, mkv_b = _memkv(mem_flat, g_mem, w_mem_b, qk_gain, l)
        xp = _ffn(xp, g_ff1, ff1_w, l)
        casts = [(w_ff2_in, w_ff2_out, l)]
        if l + 1 < depth:
            casts.append((w_ff1_in, w_ff1_out, l + 1))
        q, kvb, kvf, *cast_w = _proj(xp, g_mix, w_in_b, qk_gain, tab_p, t_p, l, casts if ride else ())
        if ride:
            ff2_w, next_ff1_w = cast_w[:3], cast_w[3:]
        else:
            ff2_w = _ffn_weights_bf16(w_ff2_in, w_ff2_out, l)
            next_ff1_w = _ffn_weights_bf16(w_ff1_in, w_ff1_out, l + 1) if l + 1 < depth else None
        xp = _mixer(xp.reshape(b_p, t_p, d), q.reshape(b_p, t_p, Q_COLS),
                    kvb.reshape(b_p, t_p, KV_COLS), mkv_b.reshape(b_p, n_mem, MEM_COLS),
                    attn_sinks, bias_a_p, bias_b_p, w_out_b, PROMPT_Q_TILE, 0, l).reshape(b_p * t_p, d)
        xp = _ffn(xp, g_ff2, ff2_w, l)
        kvf = kvf.reshape(b_p, t_p, KV_COLS)
        outs[0].append(kvf[:, t_p - keep_a:, KA_OFF:VA_OFF])
        outs[1].append(kvf[:, t_p - keep_a:, VA_OFF:KB_OFF])
        outs[2].append(kvf[:, t_p - keep_b:, KB_OFF:VB_OFF])
        outs[3].append(kvf[:, t_p - keep_b:, VB_OFF:])
        outs[4].append(mkv_f[:, :H_M * HEAD_DIM])
        outs[5].append(mkv_f[:, H_M * HEAD_DIM:])

        xs = _ffn(xs, g_ff1, ff1_w, l)
        q, kvb, kvf = _proj(xs, g_mix, w_in_b, qk_gain, tab_s, t_s, l)
        tail = ((0, 0), (0, CHUNK - t_s), (0, 0))
        kv = jnp.concatenate([past_kv[l], jnp.pad(kvb.reshape(b_s, t_s, KV_COLS), tail)], axis=1)
        y = _mixer(jnp.pad(xs.reshape(b_s, t_s, d), tail), jnp.pad(q.reshape(b_s, t_s, Q_COLS), tail),
                   kv, mkv_s[l], attn_sinks, bias_a_s, bias_b_s, w_out_b, CHUNK, BAND_ROWS, l)
        xs = _ffn(y[:, :t_s].reshape(b_s * t_s, d), g_ff2, ff2_w, l)
        outs[6].append(kvf[:, KA_OFF:VA_OFF])
        outs[7].append(kvf[:, VA_OFF:KB_OFF])
        outs[8].append(kvf[:, KB_OFF:VB_OFF])
        outs[9].append(kvf[:, VB_OFF:])
        ff1_w = next_ff1_w

    def stacked_heads(per_layer, batch):
        z = jnp.stack(per_layer)
        return z.reshape(depth, batch, -1, z.shape[-1] // HEAD_DIM, HEAD_DIM)

    return ((xp.reshape(b_p, t_p, d), xs.reshape(b_s, t_s, d))
            + tuple(stacked_heads(o, b_p) for o in outs[:6])
            + tuple(stacked_heads(o, b_s) for o in outs[6:]))
```

```python
import functools
import math

import numpy as np
import jax
import jax.numpy as jnp
from jax import lax
from jax.experimental import pallas as pl
from jax.experimental.pallas import tpu as pltpu

CHUNK = 64
HEAD_DIM = 128
H_A = 8
H_A_KV = 2
WIN_CHUNKS = 2
WINDOW = WIN_CHUNKS * CHUNK
H_B = 4
BAND_CHUNKS = 8
BAND_ROWS = BAND_CHUNKS * CHUNK
REL_CLIP = 128
H_M = 4
ROT_DIM = HEAD_DIM // 4
ROPE_THETA = 500000.0
EPS = 1e-6
PAST_LEN = 1024
LOG2E = math.log2(math.e)
Q_SCALE = HEAD_DIM ** -0.5 * LOG2E

Q_COLS = (H_A + H_B + H_M) * HEAD_DIM
KV_COLS = (2 * H_A_KV + 2 * H_B) * HEAD_DIM
MEM_COLS = 2 * H_M * HEAD_DIM
QB_OFF = H_A * HEAD_DIM
QM_OFF = (H_A + H_B) * HEAD_DIM
KA_OFF = 0
VA_OFF = H_A_KV * HEAD_DIM
KB_OFF = 2 * H_A_KV * HEAD_DIM
VB_OFF = KB_OFF + H_B * HEAD_DIM

V7X_VMEM_LIMIT_BYTES = 62 * 1024 * 1024
HEAD_GROUP = 4
FF_TILE = 512
FFN_ROW_TILE = 1024
V7X_LANES = 128
V7X_BF16_SUBLANES = 16
V7X_MXU_DIM = 256
CAST_ROW_TILE = 256
ROW_TILE = 512
PROMPT_Q_TILE = 256
MASK_VALUE = -1e30

_NT_DIMS = (((1,), (1,)), ((), ()))


def _row_tile(rows, preferred):
    tile = min(rows, preferred)
    while rows % tile:
        tile //= 2
    return tile


def _params(semantics):
    return pltpu.CompilerParams(dimension_semantics=semantics,
                                vmem_limit_bytes=V7X_VMEM_LIMIT_BYTES)


def _rms_rows(x, g):
    ms = jnp.mean(x * x, axis=-1, keepdims=True)
    return x * lax.rsqrt(ms + EPS) * g


def _layer_spec(shape, index_map, **kwargs):
    return pl.BlockSpec((None,) + tuple(shape), index_map, **kwargs)


def _ffn_kernel(x_ref, g_ref, wg_ref, wu_ref, wo_ref, o_ref, xn_ref, *, n_ff, last_cols):
    j = pl.program_id(1)

    def accumulate_onto(base_ref, cols):
        xn = xn_ref[...]
        gate = jnp.dot(xn, wg_ref[:, :cols], preferred_element_type=jnp.float32)
        up = jnp.dot(xn, wu_ref[:, :cols], preferred_element_type=jnp.float32)
        act = (gate * jax.nn.sigmoid(gate) * up * 0.5).astype(jnp.bfloat16)
        o_ref[...] = base_ref[...] + jnp.dot(act, wo_ref[:cols, :], preferred_element_type=jnp.float32)

    def normalise():
        xn_ref[...] = _rms_rows(x_ref[...], g_ref[...]).astype(xn_ref.dtype)

    @pl.when(j == 0)
    def _():
        normalise()
        accumulate_onto(x_ref, FF_TILE if n_ff > 1 else last_cols)

    if last_cols == FF_TILE:
        @pl.when(j != 0)
        def _():
            accumulate_onto(o_ref, FF_TILE)
    elif n_ff > 1:
        @pl.when((j != 0) & (j != n_ff - 1))
        def _():
            accumulate_onto(o_ref, FF_TILE)

        @pl.when(j == n_ff - 1)
        def _():
            accumulate_onto(o_ref, last_cols)


def _ffn(x, g, weights, layer):
    wg, wu, wo = weights
    rows, d = x.shape
    f = wo.shape[0]
    n_ff = pl.cdiv(f, FF_TILE)
    tm = _row_tile(rows, FFN_ROW_TILE)
    return pl.pallas_call(
        functools.partial(_ffn_kernel, n_ff=n_ff, last_cols=f - (n_ff - 1) * FF_TILE),
        grid=(rows // tm, n_ff),
        in_specs=[
            pl.BlockSpec((tm, d), lambda i, j: (i, 0)),
            _layer_spec((1, d), lambda i, j: (layer, 0, 0)),
            pl.BlockSpec((d, FF_TILE), lambda i, j: (0, j)),
            pl.BlockSpec((d, FF_TILE), lambda i, j: (0, j)),
            pl.BlockSpec((FF_TILE, d), lambda i, j: (j, 0)),
        ],
        out_specs=pl.BlockSpec((tm, d), lambda i, j: (i, 0)),
        out_shape=jax.ShapeDtypeStruct((rows, d), jnp.float32),
        scratch_shapes=[pltpu.VMEM((tm, d), jnp.bfloat16)],
        compiler_params=_params(("parallel", "arbitrary")),
        name="ffn",
    )(x, g, wg, wu, wo)


def _cast_kernel(w_ref, o_ref):
    o_ref[...] = w_ref[...].astype(o_ref.dtype)


def _ffn_weights_bf16(w_i, w_o, layer):
    _, d, two_f = w_i.shape
    f = two_f // 2
    assert f % V7X_LANES == 0
    rt = _row_tile(d, CAST_ROW_TILE)

    def half(part):
        return pl.pallas_call(
            _cast_kernel,
            grid=(d // rt,),
            in_specs=[_layer_spec((rt, f), lambda r: (layer, r, part))],
            out_specs=pl.BlockSpec((rt, f), lambda r: (r, 0)),
            out_shape=jax.ShapeDtypeStruct((d, f), jnp.bfloat16),
            compiler_params=_params(("parallel",)),
            name="cast_cols",
        )(w_i)

    return half(0), half(1), w_o[layer].astype(jnp.bfloat16)


def _head_plan():
    plan = []
    for h in range(H_A):
        plan.append((0, True, True, h * HEAD_DIM))
    for h in range(H_A_KV):
        plan.append((1, True, False, KA_OFF + h * HEAD_DIM))
    for h in range(H_A_KV):
        plan.append((None, False, False, VA_OFF + h * HEAD_DIM))
    for h in range(H_B):
        plan.append((2, False, True, QB_OFF + h * HEAD_DIM))
    for h in range(H_B):
        plan.append((3, False, False, KB_OFF + h * HEAD_DIM))
    for h in range(H_B):
        plan.append((None, False, False, VB_OFF + h * HEAD_DIM))
    for h in range(H_M):
        plan.append((4, False, True, QM_OFF + h * HEAD_DIM))
    return plan


_PLAN = _head_plan()
IN_COLS = len(_PLAN) * HEAD_DIM


def _proj_kernel(x_ref, g_ref, w_ref, qkg_ref, cos_ref, sa_ref, sb_ref, *rest, n_cast):
    cast_in = rest[:n_cast]
    q_ref, kvb_ref, kvf_ref = rest[n_cast:n_cast + 3]
    cast_out = rest[n_cast + 3:]
    for src, dst in zip(cast_in, cast_out):
        dst[...] = src[...].astype(dst.dtype)

    xn = _rms_rows(x_ref[...], g_ref[...]).astype(jnp.bfloat16)
    cos = cos_ref[...]
    sa = sa_ref[...]
    sb = sb_ref[...]
    gw = HEAD_GROUP * HEAD_DIM
    groups = sorted(range(len(_PLAN) // HEAD_GROUP),
                    key=lambda g: all(_PLAN[g * HEAD_GROUP + hh][0] is None for hh in range(HEAD_GROUP)))
    for grp in groups:
        p = jnp.dot(xn, w_ref[:, grp * gw:(grp + 1) * gw], preferred_element_type=jnp.float32)
        for hh in range(HEAD_GROUP):
            gain, rotary, is_query, col = _PLAN[grp * HEAD_GROUP + hh]
            z = p[:, hh * HEAD_DIM:(hh + 1) * HEAD_DIM]
            if gain is not None:
                z = _rms_rows(z, qkg_ref[gain:gain + 1, :])
            if rotary:
                z = (z * cos + pltpu.roll(z, HEAD_DIM - ROT_DIM // 2, 1) * sa
                     + pltpu.roll(z, ROT_DIM // 2, 1) * sb)
            cols = slice(col, col + HEAD_DIM)
            if is_query:
                q_ref[:, cols] = (z * Q_SCALE).astype(q_ref.dtype)
            else:
                kvf_ref[:, cols] = z
                kvb_ref[:, cols] = z.astype(kvb_ref.dtype)


def _rope_tables(pos):
    half = ROT_DIM // 2
    inv_freq = ROPE_THETA ** (-jnp.arange(half, dtype=jnp.float32) / half)
    ang = pos.astype(jnp.float32)[:, None] * inv_freq[None, :]
    cos = jnp.cos(ang)
    sin = jnp.sin(ang)
    t = pos.shape[0]
    ones = jnp.ones((t, HEAD_DIM - ROT_DIM), jnp.float32)
    cos_t = jnp.concatenate([cos, cos, ones], axis=1)
    sa_t = jnp.concatenate([-sin, jnp.zeros((t, HEAD_DIM - half), jnp.float32)], axis=1)
    sb_t = jnp.concatenate([jnp.zeros((t, half), jnp.float32), sin,
                            jnp.zeros((t, HEAD_DIM - ROT_DIM), jnp.float32)], axis=1)
    return cos_t, sa_t, sb_t


def _ffn_cast_fits(rows, d, f):
    n_steps = rows // _row_tile(rows, ROW_TILE)
    return (d % n_steps == 0 and (d // n_steps) % V7X_BF16_SUBLANES == 0
            and f % V7X_LANES == 0 and n_steps >= f // V7X_LANES)


def _ffn_cast_streams(w_i, w_o, layer, n_steps):
    _, d, two_f = w_i.shape
    f = two_f // 2
    rd = d // n_steps
    last = f // V7X_LANES - 1
    in_specs = [
        _layer_spec((rd, f), lambda i: (layer, i, 0)),
        _layer_spec((rd, f), lambda i: (layer, i, 1)),
        _layer_spec((V7X_LANES, d), lambda i: (layer, jnp.minimum(i, last), 0)),
    ]
    out_specs = [
        pl.BlockSpec((rd, f), lambda i: (i, 0)),
        pl.BlockSpec((rd, f), lambda i: (i, 0)),
        pl.BlockSpec((V7X_LANES, d), lambda i: (jnp.minimum(i, last), 0)),
    ]
    out_shapes = [jax.ShapeDtypeStruct((d, f), jnp.bfloat16)] * 2 + [
        jax.ShapeDtypeStruct((f, d), jnp.bfloat16)]
    return [w_i, w_i, w_o], in_specs, out_specs, out_shapes


def _proj(x, g, w_in, qk_gain, tables, seq_len, layer, casts=()):
    rows, d = x.shape
    tm = _row_tile(rows, ROW_TILE)
    cast_args, cast_in, cast_out, cast_shapes = [], [], [], []
    for w_i, w_o, cast_layer in casts:
        a, i_s, o_s, shp = _ffn_cast_streams(w_i, w_o, cast_layer, rows // tm)
        cast_args += a
        cast_in += i_s
        cast_out += o_s
        cast_shapes += shp
    if tm <= seq_len:
        assert seq_len % tm == 0
        per_seq = seq_len // tm
        tab_map = lambda i: (i % per_seq, 0)
    else:
        assert tm % seq_len == 0
        tables = tuple(jnp.tile(t, (tm // seq_len, 1)) for t in tables)
        tab_map = lambda i: (0, 0)
    tab_spec = pl.BlockSpec((tm, HEAD_DIM), tab_map)
    return pl.pallas_call(
        functools.partial(_proj_kernel, n_cast=len(cast_args)),
        grid=(rows // tm,),
        in_specs=[
            pl.BlockSpec((tm, d), lambda i: (i, 0)),
            _layer_spec((1, d), lambda i: (layer, 0, 0)),
            _layer_spec((d, IN_COLS), lambda i: (layer, 0, 0), pipeline_mode=pl.Buffered(1)),
            _layer_spec(qk_gain.shape[1:], lambda i: (layer, 0, 0)),
            tab_spec, tab_spec, tab_spec,
        ] + cast_in,
        out_specs=[
            pl.BlockSpec((tm, Q_COLS), lambda i: (i, 0)),
            pl.BlockSpec((tm, KV_COLS), lambda i: (i, 0)),
            pl.BlockSpec((tm, KV_COLS), lambda i: (i, 0)),
        ] + cast_out,
        out_shape=[
            jax.ShapeDtypeStruct((rows, Q_COLS), jnp.bfloat16),
            jax.ShapeDtypeStruct((rows, KV_COLS), jnp.bfloat16),
            jax.ShapeDtypeStruct((rows, KV_COLS), jnp.float32),
        ] + cast_shapes,
        compiler_params=_params(("arbitrary",)),
        name="proj",
    )(x, g, w_in, qk_gain, *tables, *cast_args)


def _memkv_kernel(x_ref, g_ref, w_ref, qkg_ref, of_ref, ob_ref):
    xn = _rms_rows(x_ref[...], g_ref[...]).astype(jnp.bfloat16)
    gw = H_M * HEAD_DIM
    for part in range(2):
        p = jnp.dot(xn, w_ref[:, part * gw:(part + 1) * gw], preferred_element_type=jnp.float32)
        for h in range(H_M):
            z = p[:, h * HEAD_DIM:(h + 1) * HEAD_DIM]
            if part == 0:
                z = _rms_rows(z, qkg_ref[5:6, :])
            cols = slice(part * gw + h * HEAD_DIM, part * gw + (h + 1) * HEAD_DIM)
            of_ref[:, cols] = z
            ob_ref[:, cols] = z.astype(ob_ref.dtype)


def _memkv(mem, g, w_kv, qk_gain, layer):
    rows, d = mem.shape
    tm = _row_tile(rows, ROW_TILE)
    return pl.pallas_call(
        _memkv_kernel,
        grid=(rows // tm,),
        in_specs=[
            pl.BlockSpec((tm, d), lambda i: (i, 0)),
            _layer_spec((1, d), lambda i: (layer, 0, 0)),
            _layer_spec((d, MEM_COLS), lambda i: (layer, 0, 0)),
            _layer_spec(qk_gain.shape[1:], lambda i: (layer, 0, 0)),
        ],
        out_specs=[pl.BlockSpec((tm, MEM_COLS), lambda i: (i, 0))] * 2,
        out_shape=[jax.ShapeDtypeStruct((rows, MEM_COLS), jnp.float32),
                   jax.ShapeDtypeStruct((rows, MEM_COLS), jnp.bfloat16)],
        compiler_params=_params(("parallel",)),
        name="memkv",
    )(mem, g, w_kv, qk_gain)


def _attn_kernel(sink_ref, q_ref, kv_ref, mkv_ref, ba_ref, bb_ref, x_ref, w_ref, y_ref, o_ref, *,
                 layer, q_tile, base, sub_a, pieces_b):
    tile = pl.program_id(1)
    row0 = pl.multiple_of(tile * q_tile, q_tile)
    n_sub = q_tile // sub_a
    n_var_a = ba_ref.shape[0]
    group = H_A // H_A_KV

    def piece_rows(off, size):
        start = base + row0 + off
        if base + off < 0:
            start = jnp.maximum(start, 0)
        return pl.ds(pl.multiple_of(start, math.gcd(q_tile, abs(off), base)), size)

    def window_a(u):
        start = base + row0 + u * sub_a - WINDOW
        if base + u * sub_a - WINDOW < 0:
            start = jnp.maximum(start, 0)
        return pl.ds(pl.multiple_of(start, math.gcd(sub_a, base)), WINDOW + sub_a)

    def bias_a(u):
        return ba_ref[jnp.minimum(tile * n_sub + u, n_var_a - 1)]

    jobs = []
    for g in range(H_A_KV):
        heads = range(g * group, (g + 1) * group)
        sinks = [sink_ref[layer, h] * LOG2E for h in heads]
        for u in range(n_sub):
            jobs.append((slice(u * sub_a, (u + 1) * sub_a), [h * HEAD_DIM for h in heads], kv_ref,
                         [window_a(u)], KA_OFF + g * HEAD_DIM, VA_OFF + g * HEAD_DIM,
                         functools.partial(bias_a, u), sinks))
    rows_b = [piece_rows(off, size) for off, size in pieces_b]
    for h in range(H_B):
        jobs.append((slice(0, q_tile), [QB_OFF + h * HEAD_DIM], kv_ref, rows_b, KB_OFF + h * HEAD_DIM,
                     VB_OFF + h * HEAD_DIM, functools.partial(lambda hh: bb_ref[0, hh], h), None))
    rows_m = [pl.ds(0, mkv_ref.shape[1])]
    for h in range(H_M):
        jobs.append((slice(0, q_tile), [QM_OFF + h * HEAD_DIM], mkv_ref, rows_m, h * HEAD_DIM,
                     (H_M + h) * HEAD_DIM, None, None))

    scores = []
    for q_rows, q_cols, k_ref, rows_list, k_col, _, _, _ in jobs:
        qs = [q_ref[0, q_rows, c:c + HEAD_DIM] for c in q_cols]
        q = qs[0] if len(qs) == 1 else jnp.concatenate(qs, axis=0)
        parts = [lax.dot_general(q, k_ref[0, rows, k_col:k_col + HEAD_DIM], _NT_DIMS,
                                 preferred_element_type=jnp.float32) for rows in rows_list]
        scores.append(parts[0] if len(parts) == 1 else jnp.concatenate(parts, axis=1))

    probs = []
    dens = []
    for (q_rows, q_cols, _, _, _, _, bias, sinks), s_all in zip(jobs, scores):
        n_rows = q_rows.stop - q_rows.start
        bias_val = None if bias is None else bias()
        es = []
        ds = []
        for i in range(len(q_cols)):
            s = s_all[i * n_rows:(i + 1) * n_rows]
            if bias_val is not None:
                s = s + bias_val
            m = jnp.max(s, axis=-1, keepdims=True)
            if sinks is not None:
                m = jnp.maximum(m, sinks[i])
            e = jnp.exp2(s - m)
            den = jnp.sum(e, axis=-1, keepdims=True)
            if sinks is not None:
                den = den + jnp.exp2(sinks[i] - m)
            es.append(e.astype(jnp.bfloat16))
            ds.append(den)
        probs.append(es[0] if len(es) == 1 else jnp.concatenate(es, axis=0))
        dens.append(ds)

    for (q_rows, q_cols, k_ref, rows_list, _, v_col, _, _), e, ds in zip(jobs, probs, dens):
        n_rows = q_rows.stop - q_rows.start
        acc = None
        lo = 0
        for rows in rows_list:
            pv = jnp.dot(e[:, lo:lo + rows.size], k_ref[0, rows, v_col:v_col + HEAD_DIM],
                         preferred_element_type=jnp.float32)
            acc = pv if acc is None else acc + pv
            lo += rows.size
        for i, c in enumerate(q_cols):
            o = acc[i * n_rows:(i + 1) * n_rows] / ds[i]
            o_ref[q_rows, c:c + HEAD_DIM] = o.astype(o_ref.dtype)

    y = x_ref[0]
    for c in range(0, Q_COLS, V7X_MXU_DIM):
        y = y + jnp.dot(o_ref[:, c:c + V7X_MXU_DIM], w_ref[c:c + V7X_MXU_DIM, :],
                        preferred_element_type=jnp.float32)
    y_ref[0] = y


def _window_pieces(past_rows, q_tile, base):
    size = past_rows if base >= past_rows else math.gcd(past_rows, q_tile)
    pieces = [(-past_rows + k * size, size) for k in range(past_rows // size)]
    return pieces + [(0, q_tile)]


def _leading_variants(masks):
    n_var = len(masks)
    while n_var > 1 and np.array_equal(masks[n_var - 2], masks[-1]):
        n_var -= 1
    return np.stack(masks[:n_var])


def _band_masks(q_tile, past_rows, n_past_chunks, n_tiles, first_real, new_rows):
    win = past_rows + q_tile
    r = np.arange(q_tile)[:, None] // CHUNK
    w = np.arange(win)[None, :]
    band = (w // CHUNK >= r) & (w // CHUNK <= r + n_past_chunks)
    masks = []
    for i in range(n_tiles):
        row = i * q_tile + w - past_rows
        masks.append(band & (row >= first_real) & (row < new_rows))
    return _leading_variants(masks)


def _window_masks_a(sub, n_sub_tiles, base, first_real, new_rows):
    masks = []
    for s in range(n_sub_tiles):
        q_row = s * sub + np.arange(sub)[:, None]
        k_row = max(s * sub - WINDOW, -base) + np.arange(WINDOW + sub)[None, :]
        q_chunk = q_row // CHUNK
        k_chunk = np.floor_divide(k_row, CHUNK)
        masks.append((k_chunk >= q_chunk - WIN_CHUNKS) & (k_chunk <= q_chunk)
                     & (k_row >= first_real) & (k_row < new_rows))
    return _leading_variants(masks)


def _rel_bias_tiles(rel_tab, q_tile):
    win = BAND_ROWS + q_tile
    n = win + q_tile
    k = np.arange(n)
    dist = np.where(k < win, BAND_ROWS - k, BAND_ROWS + n - k)
    seq = rel_tab[:, :, np.clip(dist, -REL_CLIP, REL_CLIP) + REL_CLIP].astype(jnp.float32) * LOG2E
    flat = jnp.tile(seq, (1, 1, q_tile))[:, :, :q_tile * (n - 1)]
    return flat.reshape(rel_tab.shape[0], H_B, q_tile, n - 1)[..., :win]


def _attn_biases(rel_tab, q_tile, n_tiles, base, past_a, past_b, new_rows):
    sub = min(q_tile, WINDOW)
    mask_a = _window_masks_a(sub, n_tiles * (q_tile // sub), base, -past_a, new_rows)
    mask_b = _band_masks(q_tile, BAND_ROWS, BAND_CHUNKS, n_tiles, -past_b, new_rows)
    bias_a = jnp.asarray(np.where(mask_a, 0.0, MASK_VALUE), jnp.float32)
    rel = _rel_bias_tiles(rel_tab, q_tile)
    bias_b = jnp.where(jnp.asarray(mask_b)[None, :, None], rel[:, None], MASK_VALUE)
    return bias_a, bias_b


def _mixer(x, q, kv, mkv, sinks, bias_a, bias_b, w_out, q_tile, base, layer):
    b, tq, d = x.shape
    assert kv.shape[1] == base + tq and q.shape[1] == tq
    n_b = bias_b.shape[1]
    win_b = BAND_ROWS + q_tile
    kern = functools.partial(_attn_kernel, layer=layer, q_tile=q_tile, base=base,
                             sub_a=bias_a.shape[1],
                             pieces_b=_window_pieces(BAND_ROWS, q_tile, base))
    return pl.pallas_call(
        kern,
        grid=(b, tq // q_tile),
        in_specs=[
            pl.BlockSpec(memory_space=pltpu.SMEM),
            pl.BlockSpec((1, q_tile, Q_COLS), lambda bi, i: (bi, i, 0)),
            pl.BlockSpec((1,) + kv.shape[1:], lambda bi, i: (bi, 0, 0)),
            pl.BlockSpec((1,) + mkv.shape[1:], lambda bi, i: (bi, 0, 0)),
            pl.BlockSpec(bias_a.shape, lambda bi, i: (0, 0, 0)),
            _layer_spec((1, H_B, q_tile, win_b),
                        lambda bi, i: (layer, jnp.minimum(i, n_b - 1), 0, 0, 0)),
            pl.BlockSpec((1, q_tile, d), lambda bi, i: (bi, i, 0)),
            _layer_spec((Q_COLS, d), lambda bi, i: (layer, 0, 0), pipeline_mode=pl.Buffered(1)),
        ],
        out_specs=pl.BlockSpec((1, q_tile, d), lambda bi, i: (bi, i, 0)),
        out_shape=jax.ShapeDtypeStruct((b, tq, d), jnp.float32),
        scratch_shapes=[pltpu.VMEM((q_tile, Q_COLS), jnp.bfloat16)],
        compiler_params=_params(("parallel", "arbitrary")),
        name="mixer",
    )(sinks, q, kv, mkv, bias_a, bias_b, x, w_out)


def kernel(x_prompt, x_sample, cache_a_k, cache_a_v, cache_b_k, cache_b_v, cache_mem_k, cache_mem_v,
           mem_prompt, norm_ff1, w_ff1_in, w_ff1_out, norm_mix, w_in, qk_gain, attn_sinks, rel_bias,
           norm_mem, w_mem_kv, w_out, norm_ff2, w_ff2_in, w_ff2_out):
    b_p, t_p, d = x_prompt.shape
    b_s, t_s, _ = x_sample.shape
    depth = w_in.shape[0]
    n_mem = mem_prompt.shape[1]
    bf16 = jnp.bfloat16
    assert t_p % PROMPT_Q_TILE == 0 and t_s <= CHUNK

    ride = _ffn_cast_fits(b_p * t_p, d, w_ff1_out.shape[1])
    ff1_w = _ffn_weights_bf16(w_ff1_in, w_ff1_out, 0)
    w_in_b = w_in.astype(bf16)
    w_out_b = w_out.astype(bf16)
    w_mem_b = w_mem_kv.astype(bf16)
    g_ff1 = norm_ff1[:, None, :]
    g_ff2 = norm_ff2[:, None, :]
    g_mix = norm_mix[:, None, :]
    g_mem = norm_mem[:, None, :]

    tab_p = _rope_tables(jnp.arange(t_p, dtype=jnp.int32))
    tab_s = _rope_tables(PAST_LEN + jnp.arange(t_s, dtype=jnp.int32))
    keep_a = min(WINDOW, t_p)
    keep_b = min(BAND_ROWS, t_p)
    la = cache_a_k.shape[2]
    lb = cache_b_k.shape[2]

    bias_a_p, bias_b_p = _attn_biases(rel_bias, PROMPT_Q_TILE, t_p // PROMPT_Q_TILE, 0, 0, 0, t_p)
    bias_a_s, bias_b_s = _attn_biases(rel_bias, CHUNK, 1, BAND_ROWS, la, lb, t_s)

    def heads_flat(c):
        return c.reshape(c.shape[:3] + (-1,)).astype(bf16)

    def front(c, rows):
        return jnp.pad(c, ((0, 0), (0, 0), (rows - c.shape[2], 0), (0, 0)))

    past_kv = jnp.concatenate(
        [front(heads_flat(cache_a_k), BAND_ROWS), front(heads_flat(cache_a_v), BAND_ROWS),
         front(heads_flat(cache_b_k), BAND_ROWS), front(heads_flat(cache_b_v), BAND_ROWS)], axis=3)
    mkv_s = jnp.concatenate([heads_flat(cache_mem_k), heads_flat(cache_mem_v)], axis=3)

    xp = x_prompt.reshape(b_p * t_p, d)
    xs = x_sample.reshape(b_s * t_s, d)
    mem_flat = mem_prompt.reshape(b_p * n_mem, d)
    outs = [[] for _ in range(10)]
    for l in range(depth):
        mkv_f, mkv_b = _memkv(mem_flat, g_mem, w_mem_b, qk_gain, l)
        xp = _ffn(xp, g_ff1, ff1_w, l)
        casts = [(w_ff2_in, w_ff2_out, l)]
        if l + 1 < depth:
            casts.append((w_ff1_in, w_ff1_out, l + 1))
        q, kvb, kvf, *cast_w = _proj(xp, g_mix, w_in_b, qk_gain, tab_p, t_p, l, casts if ride else ())
        if ride:
            ff2_w, next_ff1_w = cast_w[:3], cast_w[3:]
        else:
            ff2_w = _ffn_weights_bf16(w_ff2_in, w_ff2_out, l)
            next_ff1_w = _ffn_weights_bf16(w_ff1_in, w_ff1_out, l + 1) if l + 1 < depth else None
        xp = _mixer(xp.reshape(b_p, t_p, d), q.reshape(b_p, t_p, Q_COLS),
                    kvb.reshape(b_p, t_p, KV_COLS), mkv_b.reshape(b_p, n_mem, MEM_COLS),
                    attn_sinks, bias_a_p, bias_b_p, w_out_b, PROMPT_Q_TILE, 0, l).reshape(b_p * t_p, d)
        xp = _ffn(xp, g_ff2, ff2_w, l)
        kvf = kvf.reshape(b_p, t_p, KV_COLS)
        outs[0].append(kvf[:, t_p - keep_a:, KA_OFF:VA_OFF])
        outs[1].append(kvf[:, t_p - keep_a:, VA_OFF:KB_OFF])
        outs[2].append(kvf[:, t_p - keep_b:, KB_OFF:VB_OFF])
        outs[3].append(kvf[:, t_p - keep_b:, VB_OFF:])
        outs[4].append(mkv_f[:, :H_M * HEAD_DIM])
        outs[5].append(mkv_f[:, H_M * HEAD_DIM:])

        xs = _ffn(xs, g_ff1, ff1_w, l)
        q, kvb, kvf = _proj(xs, g_mix, w_in_b, qk_gain, tab_s, t_s, l)
        tail = ((0, 0), (0, CHUNK - t_s), (0, 0))
        kv = jnp.concatenate([past_kv[l], jnp.pad(kvb.reshape(b_s, t_s, KV_COLS), tail)], axis=1)
        y = _mixer(jnp.pad(xs.reshape(b_s, t_s, d), tail), jnp.pad(q.reshape(b_s, t_s, Q_COLS), tail),
                   kv, mkv_s[l], attn_sinks, bias_a_s, bias_b_s, w_out_b, CHUNK, BAND_ROWS, l)
        xs = _ffn(y[:, :t_s].reshape(b_s * t_s, d), g_ff2, ff2_w, l)
        outs[6].append(kvf[:, KA_OFF:VA_OFF])
        outs[7].append(kvf[:, VA_OFF:KB_OFF])
        outs[8].append(kvf[:, KB_OFF:VB_OFF])
        outs[9].append(kvf[:, VB_OFF:])
        ff1_w = next_ff1_w

    def stacked_heads(per_layer, batch):
        z = jnp.stack(per_layer)
        return z.reshape(depth, batch, -1, z.shape[-1] // HEAD_DIM, HEAD_DIM)

    return ((xp.reshape(b_p, t_p, d), xs.reshape(b_s, t_s, d))
            + tuple(stacked_heads(o, b_p) for o in outs[:6])
            + tuple(stacked_heads(o, b_s) for o in outs[6:]))
```

```python
import functools
import math

import numpy as np
import jax
import jax.numpy as jnp
from jax import lax
from jax.experimental import pallas as pl
from jax.experimental.pallas import tpu as pltpu

CHUNK = 64
HEAD_DIM = 128
H_A = 8
H_A_KV = 2
WIN_CHUNKS = 2
WINDOW = WIN_CHUNKS * CHUNK
H_B = 4
BAND_CHUNKS = 8
BAND_ROWS = BAND_CHUNKS * CHUNK
REL_CLIP = 128
H_M = 4
ROT_DIM = HEAD_DIM // 4
ROPE_THETA = 500000.0
EPS = 1e-6
PAST_LEN = 1024
LOG2E = math.log2(math.e)
Q_SCALE = HEAD_DIM ** -0.5 * LOG2E

Q_COLS = (H_A + H_B + H_M) * HEAD_DIM
KV_COLS = (2 * H_A_KV + 2 * H_B) * HEAD_DIM
MEM_COLS = 2 * H_M * HEAD_DIM
QB_OFF = H_A * HEAD_DIM
QM_OFF = (H_A + H_B) * HEAD_DIM
KA_OFF = 0
VA_OFF = H_A_KV * HEAD_DIM
KB_OFF = 2 * H_A_KV * HEAD_DIM
VB_OFF = KB_OFF + H_B * HEAD_DIM

V7X_VMEM_LIMIT_BYTES = 62 * 1024 * 1024
HEAD_GROUP = 4
FF_TILE = 512
FFN_ROW_TILE = 1024
V7X_LANES = 128
V7X_BF16_SUBLANES = 16
V7X_MXU_DIM = 256
CAST_ROW_TILE = 256
ROW_TILE = 512
PROMPT_Q_TILE = 256
MASK_VALUE = -1e30

_NT_DIMS = (((1,), (1,)), ((), ()))


def _row_tile(rows, preferred):
    tile = min(rows, preferred)
    while rows % tile:
        tile //= 2
    return tile


def _params(semantics):
    return pltpu.CompilerParams(dimension_semantics=semantics,
                                vmem_limit_bytes=V7X_VMEM_LIMIT_BYTES)


def _rms_rows(x, g):
    ms = jnp.mean(x * x, axis=-1, keepdims=True)
    return x * lax.rsqrt(ms + EPS) * g


def _layer_spec(shape, index_map, **kwargs):
    return pl.BlockSpec((None,) + tuple(shape), index_map, **kwargs)


def _ragged_step(n_ff, last_cols):
    return 1 if (last_cols != FF_TILE and n_ff > 2) else n_ff - 1


def _ff_block(j, n_ff, last_cols):
    ragged = _ragged_step(n_ff, last_cols)
    if ragged == n_ff - 1:
        return j
    return jnp.where(j == ragged, n_ff - 1, jnp.where(j < ragged, j, j - 1))


def _ffn_kernel(x_ref, g_ref, wg_ref, wu_ref, wo_ref, o_ref, xn_ref, *, n_ff, last_cols):
    j = pl.program_id(1)
    ragged = _ragged_step(n_ff, last_cols)

    def accumulate_onto(base_ref, cols):
        xn = xn_ref[...]
        gate = jnp.dot(xn, wg_ref[:, :cols], preferred_element_type=jnp.float32)
        up = jnp.dot(xn, wu_ref[:, :cols], preferred_element_type=jnp.float32)
        act = (gate * jax.nn.sigmoid(gate) * up * 0.5).astype(jnp.bfloat16)
        o_ref[...] = base_ref[...] + jnp.dot(act, wo_ref[:cols, :], preferred_element_type=jnp.float32)

    def normalise():
        xn_ref[...] = _rms_rows(x_ref[...], g_ref[...]).astype(xn_ref.dtype)

    @pl.when(j == 0)
    def _():
        normalise()
        accumulate_onto(x_ref, FF_TILE if n_ff > 1 else last_cols)

    if last_cols == FF_TILE:
        @pl.when(j != 0)
        def _():
            accumulate_onto(o_ref, FF_TILE)
    elif n_ff > 1:
        @pl.when((j != 0) & (j != ragged))
        def _():
            accumulate_onto(o_ref, FF_TILE)

        @pl.when(j == ragged)
        def _():
            accumulate_onto(o_ref, last_cols)


def _ffn(x, g, weights, layer):
    wg, wu, wo = weights
    rows, d = x.shape
    f = wo.shape[0]
    n_ff = pl.cdiv(f, FF_TILE)
    tm = _row_tile(rows, FFN_ROW_TILE)
    last_cols = f - (n_ff - 1) * FF_TILE
    block = functools.partial(_ff_block, n_ff=n_ff, last_cols=last_cols)
    return pl.pallas_call(
        functools.partial(_ffn_kernel, n_ff=n_ff, last_cols=last_cols),
        grid=(rows // tm, n_ff),
        in_specs=[
            pl.BlockSpec((tm, d), lambda i, j: (i, 0)),
            _layer_spec((1, d), lambda i, j: (layer, 0, 0)),
            pl.BlockSpec((d, FF_TILE), lambda i, j: (0, block(j))),
            pl.BlockSpec((d, FF_TILE), lambda i, j: (0, block(j))),
            pl.BlockSpec((FF_TILE, d), lambda i, j: (block(j), 0)),
        ],
        out_specs=pl.BlockSpec((tm, d), lambda i, j: (i, 0)),
        out_shape=jax.ShapeDtypeStruct((rows, d), jnp.float32),
        scratch_shapes=[pltpu.VMEM((tm, d), jnp.bfloat16)],
        compiler_params=_params(("parallel", "arbitrary")),
        name="ffn",
    )(x, g, wg, wu, wo)


def _cast_kernel(w_ref, o_ref):
    o_ref[...] = w_ref[...].astype(o_ref.dtype)


def _ffn_weights_bf16(w_i, w_o, layer):
    _, d, two_f = w_i.shape
    f = two_f // 2
    assert f % V7X_LANES == 0
    rt = _row_tile(d, CAST_ROW_TILE)

    def half(part):
        return pl.pallas_call(
            _cast_kernel,
            grid=(d // rt,),
            in_specs=[_layer_spec((rt, f), lambda r: (layer, r, part))],
            out_specs=pl.BlockSpec((rt, f), lambda r: (r, 0)),
            out_shape=jax.ShapeDtypeStruct((d, f), jnp.bfloat16),
            compiler_params=_params(("parallel",)),
            name="cast_cols",
        )(w_i)

    return half(0), half(1), w_o[layer].astype(jnp.bfloat16)


def _head_plan():
    plan = []
    for h in range(H_A):
        plan.append((0, True, True, h * HEAD_DIM))
    for h in range(H_A_KV):
        plan.append((1, True, False, KA_OFF + h * HEAD_DIM))
    for h in range(H_A_KV):
        plan.append((None, False, False, VA_OFF + h * HEAD_DIM))
    for h in range(H_B):
        plan.append((2, False, True, QB_OFF + h * HEAD_DIM))
    for h in range(H_B):
        plan.append((3, False, False, KB_OFF + h * HEAD_DIM))
    for h in range(H_B):
        plan.append((None, False, False, VB_OFF + h * HEAD_DIM))
    for h in range(H_M):
        plan.append((4, False, True, QM_OFF + h * HEAD_DIM))
    return plan


_PLAN = _head_plan()
IN_COLS = len(_PLAN) * HEAD_DIM


def _proj_kernel(x_ref, g_ref, w_ref, qkg_ref, cos_ref, sa_ref, sb_ref, *rest, n_cast):
    cast_in = rest[:n_cast]
    q_ref, kvb_ref, kvf_ref = rest[n_cast:n_cast + 3]
    cast_out = rest[n_cast + 3:]
    for src, dst in zip(cast_in, cast_out):
        dst[...] = src[...].astype(dst.dtype)

    xn = _rms_rows(x_ref[...], g_ref[...]).astype(jnp.bfloat16)
    cos = cos_ref[...]
    sa = sa_ref[...]
    sb = sb_ref[...]
    gw = HEAD_GROUP * HEAD_DIM
    groups = sorted(range(len(_PLAN) // HEAD_GROUP),
                    key=lambda g: all(_PLAN[g * HEAD_GROUP + hh][0] is None for hh in range(HEAD_GROUP)))
    for grp in groups:
        p = jnp.dot(xn, w_ref[:, grp * gw:(grp + 1) * gw], preferred_element_type=jnp.float32)
        for hh in range(HEAD_GROUP):
            gain, rotary, is_query, col = _PLAN[grp * HEAD_GROUP + hh]
            z = p[:, hh * HEAD_DIM:(hh + 1) * HEAD_DIM]
            if gain is not None:
                z = _rms_rows(z, qkg_ref[gain:gain + 1, :])
            if rotary:
                z = (z * cos + pltpu.roll(z, HEAD_DIM - ROT_DIM // 2, 1) * sa
                     + pltpu.roll(z, ROT_DIM // 2, 1) * sb)
            cols = slice(col, col + HEAD_DIM)
            if is_query:
                q_ref[:, cols] = (z * Q_SCALE).astype(q_ref.dtype)
            else:
                kvf_ref[:, cols] = z
                kvb_ref[:, cols] = z.astype(kvb_ref.dtype)


def _rope_tables(pos):
    half = ROT_DIM // 2
    inv_freq = ROPE_THETA ** (-jnp.arange(half, dtype=jnp.float32) / half)
    ang = pos.astype(jnp.float32)[:, None] * inv_freq[None, :]
    cos = jnp.cos(ang)
    sin = jnp.sin(ang)
    t = pos.shape[0]
    ones = jnp.ones((t, HEAD_DIM - ROT_DIM), jnp.float32)
    cos_t = jnp.concatenate([cos, cos, ones], axis=1)
    sa_t = jnp.concatenate([-sin, jnp.zeros((t, HEAD_DIM - half), jnp.float32)], axis=1)
    sb_t = jnp.concatenate([jnp.zeros((t, half), jnp.float32), sin,
                            jnp.zeros((t, HEAD_DIM - ROT_DIM), jnp.float32)], axis=1)
    return cos_t, sa_t, sb_t


def _ffn_cast_fits(rows, d, f):
    n_steps = rows // _row_tile(rows, ROW_TILE)
    return (d % n_steps == 0 and (d // n_steps) % V7X_BF16_SUBLANES == 0
            and f % V7X_LANES == 0 and n_steps >= f // V7X_LANES)


def _ffn_cast_streams(w_i, w_o, layer, n_steps):
    _, d, two_f = w_i.shape
    f = two_f // 2
    rd = d // n_steps
    last = f // V7X_LANES - 1
    in_specs = [
        _layer_spec((rd, f), lambda i: (layer, i, 0)),
        _layer_spec((rd, f), lambda i: (layer, i, 1)),
        _layer_spec((V7X_LANES, d), lambda i: (layer, jnp.minimum(i, last), 0)),
    ]
    out_specs = [
        pl.BlockSpec((rd, f), lambda i: (i, 0)),
        pl.BlockSpec((rd, f), lambda i: (i, 0)),
        pl.BlockSpec((V7X_LANES, d), lambda i: (jnp.minimum(i, last), 0)),
    ]
    out_shapes = [jax.ShapeDtypeStruct((d, f), jnp.bfloat16)] * 2 + [
        jax.ShapeDtypeStruct((f, d), jnp.bfloat16)]
    return [w_i, w_i, w_o], in_specs, out_specs, out_shapes


def _proj(x, g, w_in, qk_gain, tables, seq_len, layer, casts=()):
    rows, d = x.shape
    tm = _row_tile(rows, ROW_TILE)
    cast_args, cast_in, cast_out, cast_shapes = [], [], [], []
    for w_i, w_o, cast_layer in casts:
        a, i_s, o_s, shp = _ffn_cast_streams(w_i, w_o, cast_layer, rows // tm)
        cast_args += a
        cast_in += i_s
        cast_out += o_s
        cast_shapes += shp
    if tm <= seq_len:
        assert seq_len % tm == 0
        per_seq = seq_len // tm
        tab_map = lambda i: (i % per_seq, 0)
    else:
        assert tm % seq_len == 0
        tables = tuple(jnp.tile(t, (tm // seq_len, 1)) for t in tables)
        tab_map = lambda i: (0, 0)
    tab_spec = pl.BlockSpec((tm, HEAD_DIM), tab_map)
    return pl.pallas_call(
        functools.partial(_proj_kernel, n_cast=len(cast_args)),
        grid=(rows // tm,),
        in_specs=[
            pl.BlockSpec((tm, d), lambda i: (i, 0)),
            _layer_spec((1, d), lambda i: (layer, 0, 0)),
            _layer_spec((d, IN_COLS), lambda i: (layer, 0, 0), pipeline_mode=pl.Buffered(1)),
            _layer_spec(qk_gain.shape[1:], lambda i: (layer, 0, 0)),
            tab_spec, tab_spec, tab_spec,
        ] + cast_in,
        out_specs=[
            pl.BlockSpec((tm, Q_COLS), lambda i: (i, 0)),
            pl.BlockSpec((tm, KV_COLS), lambda i: (i, 0)),
            pl.BlockSpec((tm, KV_COLS), lambda i: (i, 0)),
        ] + cast_out,
        out_shape=[
            jax.ShapeDtypeStruct((rows, Q_COLS), jnp.bfloat16),
            jax.ShapeDtypeStruct((rows, KV_COLS), jnp.bfloat16),
            jax.ShapeDtypeStruct((rows, KV_COLS), jnp.float32),
        ] + cast_shapes,
        compiler_params=_params(("arbitrary",)),
        name="proj",
    )(x, g, w_in, qk_gain, *tables, *cast_args)


def _memkv_kernel(x_ref, g_ref, w_ref, qkg_ref, of_ref, ob_ref):
    xn = _rms_rows(x_ref[...], g_ref[...]).astype(jnp.bfloat16)
    gw = H_M * HEAD_DIM
    for part in range(2):
        p = jnp.dot(xn, w_ref[:, part * gw:(part + 1) * gw], preferred_element_type=jnp.float32)
        for h in range(H_M):
            z = p[:, h * HEAD_DIM:(h + 1) * HEAD_DIM]
            if part == 0:
                z = _rms_rows(z, qkg_ref[5:6, :])
            cols = slice(part * gw + h * HEAD_DIM, part * gw + (h + 1) * HEAD_DIM)
            of_ref[:, cols] = z
            ob_ref[:, cols] = z.astype(ob_ref.dtype)


def _memkv(mem, g, w_kv, qk_gain, layer):
    rows, d = mem.shape
    tm = _row_tile(rows, ROW_TILE)
    return pl.pallas_call(
        _memkv_kernel,
        grid=(rows // tm,),
        in_specs=[
            pl.BlockSpec((tm, d), lambda i: (i, 0)),
            _layer_spec((1, d), lambda i: (layer, 0, 0)),
            _layer_spec((d, MEM_COLS), lambda i: (layer, 0, 0)),
            _layer_spec(qk_gain.shape[1:], lambda i: (layer, 0, 0)),
        ],
        out_specs=[pl.BlockSpec((tm, MEM_COLS), lambda i: (i, 0))] * 2,
        out_shape=[jax.ShapeDtypeStruct((rows, MEM_COLS), jnp.float32),
                   jax.ShapeDtypeStruct((rows, MEM_COLS), jnp.bfloat16)],
        compiler_params=_params(("parallel",)),
        name="memkv",
    )(mem, g, w_kv, qk_gain)


def _attn_kernel(sink_ref, q_ref, kv_ref, mkv_ref, ba_ref, bb_ref, x_ref, w_ref, y_ref, o_ref, *,
                 layer, q_tile, base, sub_a, pieces_b):
    tile = pl.program_id(1)
    row0 = pl.multiple_of(tile * q_tile, q_tile)
    n_sub = q_tile // sub_a
    n_var_a = ba_ref.shape[0]
    group = H_A // H_A_KV

    def piece_rows(off, size):
        start = base + row0 + off
        if base + off < 0:
            start = jnp.maximum(start, 0)
        return pl.ds(pl.multiple_of(start, math.gcd(q_tile, abs(off), base)), size)

    def window_a(u):
        start = base + row0 + u * sub_a - WINDOW
        if base + u * sub_a - WINDOW < 0:
            start = jnp.maximum(start, 0)
        return pl.ds(pl.multiple_of(start, math.gcd(sub_a, base)), WINDOW + sub_a)

    def bias_a(u):
        return ba_ref[jnp.minimum(tile * n_sub + u, n_var_a - 1)]

    jobs = []
    for g in range(H_A_KV):
        heads = range(g * group, (g + 1) * group)
        sinks = [sink_ref[layer, h] * LOG2E for h in heads]
        for u in range(n_sub):
            jobs.append((slice(u * sub_a, (u + 1) * sub_a), [h * HEAD_DIM for h in heads], kv_ref,
                         [window_a(u)], KA_OFF + g * HEAD_DIM, VA_OFF + g * HEAD_DIM,
                         functools.partial(bias_a, u), sinks))
    rows_b = [piece_rows(off, size) for off, size in pieces_b]
    for h in range(H_B):
        jobs.append((slice(0, q_tile), [QB_OFF + h * HEAD_DIM], kv_ref, rows_b, KB_OFF + h * HEAD_DIM,
                     VB_OFF + h * HEAD_DIM, functools.partial(lambda hh: bb_ref[0, hh], h), None))
    rows_m = [pl.ds(0, mkv_ref.shape[1])]
    for h in range(H_M):
        jobs.append((slice(0, q_tile), [QM_OFF + h * HEAD_DIM], mkv_ref, rows_m, h * HEAD_DIM,
                     (H_M + h) * HEAD_DIM, None, None))

    scores = []
    for q_rows, q_cols, k_ref, rows_list, k_col, _, _, _ in jobs:
        qs = [q_ref[0, q_rows, c:c + HEAD_DIM] for c in q_cols]
        q = qs[0] if len(qs) == 1 else jnp.concatenate(qs, axis=0)
        parts = [lax.dot_general(q, k_ref[0, rows, k_col:k_col + HEAD_DIM], _NT_DIMS,
                                 preferred_element_type=jnp.float32) for rows in rows_list]
        scores.append(parts[0] if len(parts) == 1 else jnp.concatenate(parts, axis=1))

    probs = []
    dens = []
    for (q_rows, q_cols, _, _, _, _, bias, sinks), s_all in zip(jobs, scores):
        n_rows = q_rows.stop - q_rows.start
        bias_val = None if bias is None else bias()
        es = []
        ds = []
        for i in range(len(q_cols)):
            s = s_all[i * n_rows:(i + 1) * n_rows]
            if bias_val is not None:
                s = s + bias_val
            m = jnp.max(s, axis=-1, keepdims=True)
            if sinks is not None:
                m = jnp.maximum(m, sinks[i])
            e = jnp.exp2(s - m)
            den = jnp.sum(e, axis=-1, keepdims=True)
            if sinks is not None:
                den = den + jnp.exp2(sinks[i] - m)
            es.append(e.astype(jnp.bfloat16))
            ds.append(den)
        probs.append(es[0] if len(es) == 1 else jnp.concatenate(es, axis=0))
        dens.append(ds)

    def weighted_values(job, e, ds):
        q_rows, q_cols, k_ref, rows_list, _, v_col, _, _ = job
        n_rows = q_rows.stop - q_rows.start
        acc = None
        lo = 0
        for rows in rows_list:
            pv = jnp.dot(e[:, lo:lo + rows.size], k_ref[0, rows, v_col:v_col + HEAD_DIM],
                         preferred_element_type=jnp.float32)
            acc = pv if acc is None else acc + pv
            lo += rows.size
        for i, c in enumerate(q_cols):
            o = acc[i * n_rows:(i + 1) * n_rows] / ds[i]
            o_ref[q_rows, c:c + HEAD_DIM] = o.astype(o_ref.dtype)

    for job, e, ds in zip(jobs, probs, dens):
        weighted_values(job, e, ds)

    y = x_ref[0]
    for c in range(0, Q_COLS, V7X_MXU_DIM):
        y = y + jnp.dot(o_ref[:, c:c + V7X_MXU_DIM], w_ref[c:c + V7X_MXU_DIM, :],
                        preferred_element_type=jnp.float32)
    y_ref[0] = y


def _window_pieces(past_rows, q_tile, base):
    size = past_rows if base >= past_rows else math.gcd(past_rows, q_tile)
    pieces = [(-past_rows + k * size, size) for k in range(past_rows // size)]
    return pieces + [(0, q_tile)]


def _leading_variants(masks):
    n_var = len(masks)
    while n_var > 1 and np.array_equal(masks[n_var - 2], masks[-1]):
        n_var -= 1
    return np.stack(masks[:n_var])


def _band_masks(q_tile, past_rows, n_past_chunks, n_tiles, first_real, new_rows):
    win = past_rows + q_tile
    r = np.arange(q_tile)[:, None] // CHUNK
    w = np.arange(win)[None, :]
    band = (w // CHUNK >= r) & (w // CHUNK <= r + n_past_chunks)
    masks = []
    for i in range(n_tiles):
        row = i * q_tile + w - past_rows
        masks.append(band & (row >= first_real) & (row < new_rows))
    return _leading_variants(masks)


def _window_masks_a(sub, n_sub_tiles, base, first_real, new_rows):
    masks = []
    for s in range(n_sub_tiles):
        q_row = s * sub + np.arange(sub)[:, None]
        k_row = max(s * sub - WINDOW, -base) + np.arange(WINDOW + sub)[None, :]
        q_chunk = q_row // CHUNK
        k_chunk = np.floor_divide(k_row, CHUNK)
        masks.append((k_chunk >= q_chunk - WIN_CHUNKS) & (k_chunk <= q_chunk)
                     & (k_row >= first_real) & (k_row < new_rows))
    return _leading_variants(masks)


def _rel_bias_tiles(rel_tab, q_tile):
    win = BAND_ROWS + q_tile
    n = win + q_tile
    k = np.arange(n)
    dist = np.where(k < win, BAND_ROWS - k, BAND_ROWS + n - k)
    seq = rel_tab[:, :, np.clip(dist, -REL_CLIP, REL_CLIP) + REL_CLIP].astype(jnp.float32) * LOG2E
    flat = jnp.tile(seq, (1, 1, q_tile))[:, :, :q_tile * (n - 1)]
    return flat.reshape(rel_tab.shape[0], H_B, q_tile, n - 1)[..., :win]


def _attn_biases(rel_tab, q_tile, n_tiles, base, past_a, past_b, new_rows):
    sub = min(q_tile, WINDOW)
    mask_a = _window_masks_a(sub, n_tiles * (q_tile // sub), base, -past_a, new_rows)
    mask_b = _band_masks(q_tile, BAND_ROWS, BAND_CHUNKS, n_tiles, -past_b, new_rows)
    bias_a = jnp.asarray(np.where(mask_a, 0.0, MASK_VALUE), jnp.float32)
    rel = _rel_bias_tiles(rel_tab, q_tile)
    bias_b = jnp.where(jnp.asarray(mask_b)[None, :, None], rel[:, None], MASK_VALUE)
    return bias_a, bias_b


def _mixer(x, q, kv, mkv, sinks, bias_a, bias_b, w_out, q_tile, base, layer):
    b, tq, d = x.shape
    assert kv.shape[1] == base + tq and q.shape[1] == tq
    n_b = bias_b.shape[1]
    win_b = BAND_ROWS + q_tile
    kern = functools.partial(_attn_kernel, layer=layer, q_tile=q_tile, base=base,
                             sub_a=bias_a.shape[1],
                             pieces_b=_window_pieces(BAND_ROWS, q_tile, base))
    return pl.pallas_call(
        kern,
        grid=(b, tq // q_tile),
        in_specs=[
            pl.BlockSpec(memory_space=pltpu.SMEM),
            pl.BlockSpec((1, q_tile, Q_COLS), lambda bi, i: (bi, i, 0)),
            pl.BlockSpec((1,) + kv.shape[1:], lambda bi, i: (bi, 0, 0)),
            pl.BlockSpec((1,) + mkv.shape[1:], lambda bi, i: (bi, 0, 0)),
            pl.BlockSpec(bias_a.shape, lambda bi, i: (0, 0, 0)),
            _layer_spec((1, H_B, q_tile, win_b),
                        lambda bi, i: (layer, jnp.minimum(i, n_b - 1), 0, 0, 0)),
            pl.BlockSpec((1, q_tile, d), lambda bi, i: (bi, i, 0)),
            _layer_spec((Q_COLS, d), lambda bi, i: (layer, 0, 0), pipeline_mode=pl.Buffered(1)),
        ],
        out_specs=pl.BlockSpec((1, q_tile, d), lambda bi, i: (bi, i, 0)),
        out_shape=jax.ShapeDtypeStruct((b, tq, d), jnp.float32),
        scratch_shapes=[pltpu.VMEM((q_tile, Q_COLS), jnp.bfloat16)],
        compiler_params=_params(("parallel", "arbitrary")),
        name="mixer",
    )(sinks, q, kv, mkv, bias_a, bias_b, x, w_out)


def kernel(x_prompt, x_sample, cache_a_k, cache_a_v, cache_b_k, cache_b_v, cache_mem_k, cache_mem_v,
           mem_prompt, norm_ff1, w_ff1_in, w_ff1_out, norm_mix, w_in, qk_gain, attn_sinks, rel_bias,
           norm_mem, w_mem_kv, w_out, norm_ff2, w_ff2_in, w_ff2_out):
    b_p, t_p, d = x_prompt.shape
    b_s, t_s, _ = x_sample.shape
    depth = w_in.shape[0]
    n_mem = mem_prompt.shape[1]
    bf16 = jnp.bfloat16
    assert t_p % PROMPT_Q_TILE == 0 and t_s <= CHUNK

    ride = _ffn_cast_fits(b_p * t_p, d, w_ff1_out.shape[1])
    ff1_w = _ffn_weights_bf16(w_ff1_in, w_ff1_out, 0)
    w_in_b = w_in.astype(bf16)
    w_out_b = w_out.astype(bf16)
    w_mem_b = w_mem_kv.astype(bf16)
    g_ff1 = norm_ff1[:, None, :]
    g_ff2 = norm_ff2[:, None, :]
    g_mix = norm_mix[:, None, :]
    g_mem = norm_mem[:, None, :]

    tab_p = _rope_tables(jnp.arange(t_p, dtype=jnp.int32))
    tab_s = _rope_tables(PAST_LEN + jnp.arange(t_s, dtype=jnp.int32))
    keep_a = min(WINDOW, t_p)
    keep_b = min(BAND_ROWS, t_p)
    la = cache_a_k.shape[2]
    lb = cache_b_k.shape[2]

    bias_a_p, bias_b_p = _attn_biases(rel_bias, PROMPT_Q_TILE, t_p // PROMPT_Q_TILE, 0, 0, 0, t_p)
    bias_a_s, bias_b_s = _attn_biases(rel_bias, CHUNK, 1, BAND_ROWS, la, lb, t_s)

    def heads_flat(c):
        return c.reshape(c.shape[:3] + (-1,)).astype(bf16)

    def front(c, rows):
        return jnp.pad(c, ((0, 0), (0, 0), (rows - c.shape[2], 0), (0, 0)))

    past_kv = jnp.concatenate(
        [front(heads_flat(cache_a_k), BAND_ROWS), front(heads_flat(cache_a_v), BAND_ROWS),
         front(heads_flat(cache_b_k), BAND_ROWS), front(heads_flat(cache_b_v), BAND_ROWS)], axis=3)
    mkv_s = jnp.concatenate([heads_flat(cache_mem_k), heads_flat(cache_mem_v)], axis=3)

    xp = x_prompt.reshape(b_p * t_p, d)
    xs = x_sample.reshape(b_s * t_s, d)
    mem_flat = mem_prompt.reshape(b_p * n_mem, d)
    outs = [[] for _ in range(10)]
    for l in range(depth):
        mkv_f, mkv_b = _memkv(mem_flat, g_mem, w_mem_b, qk_gain, l)
        xp = _ffn(xp, g_ff1, ff1_w, l)
        casts = [(w_ff2_in, w_ff2_out, l)]
        if l + 1 < depth:
            casts.append((w_ff1_in, w_ff1_out, l + 1))
        q, kvb, kvf, *cast_w = _proj(xp, g_mix, w_in_b, qk_gain, tab_p, t_p, l, casts if ride else ())
        if ride:
            ff2_w, next_ff1_w = cast_w[:3], cast_w[3:]
        else:
            ff2_w = _ffn_weights_bf16(w_ff2_in, w_ff2_out, l)
            next_ff1_w = _ffn_weights_bf16(w_ff1_in, w_ff1_out, l + 1) if l + 1 < depth else None
        xp = _mixer(xp.reshape(b_p, t_p, d), q.reshape(b_p, t_p, Q_COLS),
                    kvb.reshape(b_p, t_p, KV_COLS), mkv_b.reshape(b_p, n_mem, MEM_COLS),
                    attn_sinks, bias_a_p, bias_b_p, w_out_b, PROMPT_Q_TILE, 0, l).reshape(b_p * t_p, d)
        xp = _ffn(xp, g_ff2, ff2_w, l)
        kvf = kvf.reshape(b_p, t_p, KV_COLS)
        outs[0].append(kvf[:, t_p - keep_a:, KA_OFF:VA_OFF])
        outs[1].append(kvf[:, t_p - keep_a:, VA_OFF:KB_OFF])
        outs[2].append(kvf[:, t_p - keep_b:, KB_OFF:VB_OFF])
        outs[3].append(kvf[:, t_p - keep_b:, VB_OFF:])
        outs[4].append(mkv_f[:, :H_M * HEAD_DIM])
        outs[5].append(mkv_f[:, H_M * HEAD_DIM:])

        xs = _ffn(xs, g_ff1, ff1_w, l)
        q, kvb, kvf = _proj(xs, g_mix, w_in_b, qk_gain, tab_s, t_s, l)
        tail = ((0, 0), (0, CHUNK - t_s), (0, 0))
        kv = jnp.concatenate([past_kv[l], jnp.pad(kvb.reshape(b_s, t_s, KV_COLS), tail)], axis=1)
        y = _mixer(jnp.pad(xs.reshape(b_s, t_s, d), tail), jnp.pad(q.reshape(b_s, t_s, Q_COLS), tail),
                   kv, mkv_s[l], attn_sinks, bias_a_s, bias_b_s, w_out_b, CHUNK, BAND_ROWS, l)
        xs = _ffn(y[:, :t_s].reshape(b_s * t_s, d), g_ff2, ff2_w, l)
        outs[6].append(kvf[:, KA_OFF:VA_OFF])
        outs[7].append(kvf[:, VA_OFF:KB_OFF])
        outs[8].append(kvf[:, KB_OFF:VB_OFF])
        outs[9].append(kvf[:, VB_OFF:])
        ff1_w = next_ff1_w

    def stacked_heads(per_layer, batch):
        z = jnp.stack(per_layer)
        return z.reshape(depth, batch, -1, z.shape[-1] // HEAD_DIM, HEAD_DIM)

    return ((xp.reshape(b_p, t_p, d), xs.reshape(b_s, t_s, d))
            + tuple(stacked_heads(o, b_p) for o in outs[:6])
            + tuple(stacked_heads(o, b_s) for o in outs[6:]))
```

```python
import functools
import math

import numpy as np
import jax
import jax.numpy as jnp
from jax import lax
from jax.experimental import pallas as pl
from jax.experimental.pallas import tpu as pltpu

CHUNK = 64
HEAD_DIM = 128
H_A = 8
H_A_KV = 2
WIN_CHUNKS = 2
WINDOW = WIN_CHUNKS * CHUNK
H_B = 4
BAND_CHUNKS = 8
BAND_ROWS = BAND_CHUNKS * CHUNK
REL_CLIP = 128
H_M = 4
ROT_DIM = HEAD_DIM // 4
ROPE_THETA = 500000.0
EPS = 1e-6
PAST_LEN = 1024
LOG2E = math.log2(math.e)
Q_SCALE = HEAD_DIM ** -0.5 * LOG2E

Q_COLS = (H_A + H_B + H_M) * HEAD_DIM
KV_COLS = (2 * H_A_KV + 2 * H_B) * HEAD_DIM
MEM_COLS = 2 * H_M * HEAD_DIM
QB_OFF = H_A * HEAD_DIM
QM_OFF = (H_A + H_B) * HEAD_DIM
KA_OFF = 0
VA_OFF = H_A_KV * HEAD_DIM
KB_OFF = 2 * H_A_KV * HEAD_DIM
VB_OFF = KB_OFF + H_B * HEAD_DIM

V7X_VMEM_LIMIT_BYTES = 62 * 1024 * 1024
HEAD_GROUP = 4
FF_TILE = 1024
FFN_ROW_TILE = 1024
V7X_LANES = 128
V7X_BF16_SUBLANES = 16
V7X_MXU_DIM = 256
CAST_ROW_TILE = 256
ROW_TILE = 512
PROMPT_Q_TILE = 256
MASK_VALUE = -1e30

_NT_DIMS = (((1,), (1,)), ((), ()))


def _row_tile(rows, preferred):
    tile = min(rows, preferred)
    while rows % tile:
        tile //= 2
    return tile


def _params(semantics):
    return pltpu.CompilerParams(dimension_semantics=semantics,
                                vmem_limit_bytes=V7X_VMEM_LIMIT_BYTES)


def _rms_rows(x, g):
    ms = jnp.mean(x * x, axis=-1, keepdims=True)
    return x * lax.rsqrt(ms + EPS) * g


def _layer_spec(shape, index_map, **kwargs):
    return pl.BlockSpec((None,) + tuple(shape), index_map, **kwargs)


def _ragged_step(n_ff, last_cols):
    return 1 if (last_cols != FF_TILE and n_ff > 2) else n_ff - 1


def _ff_block(j, n_ff, last_cols):
    ragged = _ragged_step(n_ff, last_cols)
    if ragged == n_ff - 1:
        return j
    return jnp.where(j == ragged, n_ff - 1, jnp.where(j < ragged, j, j - 1))


def _ffn_kernel(x_hbm, g_ref, wg_ref, wu_ref, wo_ref, o_ref, xn_ref, x_ref, x_sem, *, n_ff, last_cols):
    i = pl.program_id(0)
    j = pl.program_id(1)
    n_tiles = pl.num_programs(0)
    tm = x_ref.shape[0]
    ragged = _ragged_step(n_ff, last_cols)

    def x_copy(tile):
        rows = pl.ds(pl.multiple_of(tile * tm, tm), tm)
        return pltpu.make_async_copy(x_hbm.at[rows, :], x_ref, x_sem)

    @pl.when((j == 0) if n_ff == 1 else (i == 0) & (j == 0))
    def _():
        x_copy(i).start()

    @pl.when(j == 0)
    def _():
        x_copy(i).wait()

    if n_ff > 1:
        @pl.when((j == 1) & (i + 1 < n_tiles))
        def _():
            x_copy(i + 1).start()

    def accumulate_onto(base_ref, cols):
        xn = xn_ref[...]
        gate = jnp.dot(xn, wg_ref[:, :cols], preferred_element_type=jnp.float32)
        up = jnp.dot(xn, wu_ref[:, :cols], preferred_element_type=jnp.float32)
        act = (gate * jax.nn.sigmoid(gate) * up * 0.5).astype(jnp.bfloat16)
        o_ref[...] = base_ref[...] + jnp.dot(act, wo_ref[:cols, :], preferred_element_type=jnp.float32)

    def normalise():
        xn_ref[...] = _rms_rows(x_ref[...], g_ref[...]).astype(xn_ref.dtype)

    @pl.when(j == 0)
    def _():
        normalise()
        accumulate_onto(x_ref, FF_TILE if n_ff > 1 else last_cols)

    if last_cols == FF_TILE:
        @pl.when(j != 0)
        def _():
            accumulate_onto(o_ref, FF_TILE)
    elif n_ff > 1:
        @pl.when((j != 0) & (j != ragged))
        def _():
            accumulate_onto(o_ref, FF_TILE)

        @pl.when(j == ragged)
        def _():
            accumulate_onto(o_ref, last_cols)


def _ffn(x, g, weights, layer):
    wg, wu, wo = weights
    rows, d = x.shape
    f = wo.shape[0]
    n_ff = pl.cdiv(f, FF_TILE)
    tm = _row_tile(rows, FFN_ROW_TILE)
    last_cols = f - (n_ff - 1) * FF_TILE
    block = functools.partial(_ff_block, n_ff=n_ff, last_cols=last_cols)
    return pl.pallas_call(
        functools.partial(_ffn_kernel, n_ff=n_ff, last_cols=last_cols),
        grid=(rows // tm, n_ff),
        in_specs=[
            pl.BlockSpec(memory_space=pl.ANY),
            _layer_spec((1, d), lambda i, j: (layer, 0, 0)),
            pl.BlockSpec((d, FF_TILE), lambda i, j: (0, block(j))),
            pl.BlockSpec((d, FF_TILE), lambda i, j: (0, block(j))),
            pl.BlockSpec((FF_TILE, d), lambda i, j: (block(j), 0)),
        ],
        out_specs=pl.BlockSpec((tm, d), lambda i, j: (i, 0)),
        out_shape=jax.ShapeDtypeStruct((rows, d), jnp.float32),
        scratch_shapes=[pltpu.VMEM((tm, d), jnp.bfloat16), pltpu.VMEM((tm, d), jnp.float32),
                        pltpu.SemaphoreType.DMA(())],
        compiler_params=_params(("arbitrary", "arbitrary")),
        name="ffn",
    )(x, g, wg, wu, wo)


def _cast_kernel(w_ref, o_ref):
    o_ref[...] = w_ref[...].astype(o_ref.dtype)


def _ffn_weights_bf16(w_i, w_o, layer):
    _, d, two_f = w_i.shape
    f = two_f // 2
    assert f % V7X_LANES == 0
    rt = _row_tile(d, CAST_ROW_TILE)

    def half(part):
        return pl.pallas_call(
            _cast_kernel,
            grid=(d // rt,),
            in_specs=[_layer_spec((rt, f), lambda r: (layer, r, part))],
            out_specs=pl.BlockSpec((rt, f), lambda r: (r, 0)),
            out_shape=jax.ShapeDtypeStruct((d, f), jnp.bfloat16),
            compiler_params=_params(("parallel",)),
            name="cast_cols",
        )(w_i)

    return half(0), half(1), w_o[layer].astype(jnp.bfloat16)


def _head_plan():
    plan = []
    for h in range(H_A):
        plan.append((0, True, True, h * HEAD_DIM))
    for h in range(H_A_KV):
        plan.append((1, True, False, KA_OFF + h * HEAD_DIM))
    for h in range(H_A_KV):
        plan.append((None, False, False, VA_OFF + h * HEAD_DIM))
    for h in range(H_B):
        plan.append((2, False, True, QB_OFF + h * HEAD_DIM))
    for h in range(H_B):
        plan.append((3, False, False, KB_OFF + h * HEAD_DIM))
    for h in range(H_B):
        plan.append((None, False, False, VB_OFF + h * HEAD_DIM))
    for h in range(H_M):
        plan.append((4, False, True, QM_OFF + h * HEAD_DIM))
    return plan


_PLAN = _head_plan()
IN_COLS = len(_PLAN) * HEAD_DIM


def _proj_kernel(x_ref, g_ref, w_ref, qkg_ref, cos_ref, sa_ref, sb_ref, *rest, n_cast):
    cast_in = rest[:n_cast]
    q_ref, kvb_ref, kvf_ref = rest[n_cast:n_cast + 3]
    cast_out = rest[n_cast + 3:]
    for src, dst in zip(cast_in, cast_out):
        dst[...] = src[...].astype(dst.dtype)

    xn = _rms_rows(x_ref[...], g_ref[...]).astype(jnp.bfloat16)
    cos = cos_ref[...]
    sa = sa_ref[...]
    sb = sb_ref[...]
    gw = HEAD_GROUP * HEAD_DIM
    groups = sorted(range(len(_PLAN) // HEAD_GROUP),
                    key=lambda g: all(_PLAN[g * HEAD_GROUP + hh][0] is None for hh in range(HEAD_GROUP)))
    for grp in groups:
        p = jnp.dot(xn, w_ref[:, grp * gw:(grp + 1) * gw], preferred_element_type=jnp.float32)
        for hh in range(HEAD_GROUP):
            gain, rotary, is_query, col = _PLAN[grp * HEAD_GROUP + hh]
            z = p[:, hh * HEAD_DIM:(hh + 1) * HEAD_DIM]
            if gain is not None:
                z = _rms_rows(z, qkg_ref[gain:gain + 1, :])
            if rotary:
                z = (z * cos + pltpu.roll(z, HEAD_DIM - ROT_DIM // 2, 1) * sa
                     + pltpu.roll(z, ROT_DIM // 2, 1) * sb)
            cols = slice(col, col + HEAD_DIM)
            if is_query:
                q_ref[:, cols] = (z * Q_SCALE).astype(q_ref.dtype)
            else:
                kvf_ref[:, cols] = z
                kvb_ref[:, cols] = z.astype(kvb_ref.dtype)


def _rope_tables(pos):
    half = ROT_DIM // 2
    inv_freq = ROPE_THETA ** (-jnp.arange(half, dtype=jnp.float32) / half)
    ang = pos.astype(jnp.float32)[:, None] * inv_freq[None, :]
    cos = jnp.cos(ang)
    sin = jnp.sin(ang)
    t = pos.shape[0]
    ones = jnp.ones((t, HEAD_DIM - ROT_DIM), jnp.float32)
    cos_t = jnp.concatenate([cos, cos, ones], axis=1)
    sa_t = jnp.concatenate([-sin, jnp.zeros((t, HEAD_DIM - half), jnp.float32)], axis=1)
    sb_t = jnp.concatenate([jnp.zeros((t, half), jnp.float32), sin,
                            jnp.zeros((t, HEAD_DIM - ROT_DIM), jnp.float32)], axis=1)
    return cos_t, sa_t, sb_t


def _ffn_cast_fits(rows, d, f):
    n_steps = rows // _row_tile(rows, ROW_TILE)
    return (d % n_steps == 0 and (d // n_steps) % V7X_BF16_SUBLANES == 0
            and f % V7X_LANES == 0 and n_steps >= f // V7X_LANES)


def _ffn_cast_streams(w_i, w_o, layer, n_steps):
    _, d, two_f = w_i.shape
    f = two_f // 2
    rd = d // n_steps
    last = f // V7X_LANES - 1
    in_specs = [
        _layer_spec((rd, f), lambda i: (layer, i, 0)),
        _layer_spec((rd, f), lambda i: (layer, i, 1)),
        _layer_spec((V7X_LANES, d), lambda i: (layer, jnp.minimum(i, last), 0)),
    ]
    out_specs = [
        pl.BlockSpec((rd, f), lambda i: (i, 0)),
        pl.BlockSpec((rd, f), lambda i: (i, 0)),
        pl.BlockSpec((V7X_LANES, d), lambda i: (jnp.minimum(i, last), 0)),
    ]
    out_shapes = [jax.ShapeDtypeStruct((d, f), jnp.bfloat16)] * 2 + [
        jax.ShapeDtypeStruct((f, d), jnp.bfloat16)]
    return [w_i, w_i, w_o], in_specs, out_specs, out_shapes


def _proj(x, g, w_in, qk_gain, tables, seq_len, layer, casts=()):
    rows, d = x.shape
    tm = _row_tile(rows, ROW_TILE)
    cast_args, cast_in, cast_out, cast_shapes = [], [], [], []
    for w_i, w_o, cast_layer in casts:
        a, i_s, o_s, shp = _ffn_cast_streams(w_i, w_o, cast_layer, rows // tm)
        cast_args += a
        cast_in += i_s
        cast_out += o_s
        cast_shapes += shp
    if tm <= seq_len:
        assert seq_len % tm == 0
        per_seq = seq_len // tm
        tab_map = lambda i: (i % per_seq, 0)
    else:
        assert tm % seq_len == 0
        tables = tuple(jnp.tile(t, (tm // seq_len, 1)) for t in tables)
        tab_map = lambda i: (0, 0)
    tab_spec = pl.BlockSpec((tm, HEAD_DIM), tab_map)
    return pl.pallas_call(
        functools.partial(_proj_kernel, n_cast=len(cast_args)),
        grid=(rows // tm,),
        in_specs=[
            pl.BlockSpec((tm, d), lambda i: (i, 0)),
            _layer_spec((1, d), lambda i: (layer, 0, 0)),
            _layer_spec((d, IN_COLS), lambda i: (layer, 0, 0), pipeline_mode=pl.Buffered(1)),
            _layer_spec(qk_gain.shape[1:], lambda i: (layer, 0, 0)),
            tab_spec, tab_spec, tab_spec,
        ] + cast_in,
        out_specs=[
            pl.BlockSpec((tm, Q_COLS), lambda i: (i, 0)),
            pl.BlockSpec((tm, KV_COLS), lambda i: (i, 0)),
            pl.BlockSpec((tm, KV_COLS), lambda i: (i, 0)),
        ] + cast_out,
        out_shape=[
            jax.ShapeDtypeStruct((rows, Q_COLS), jnp.bfloat16),
            jax.ShapeDtypeStruct((rows, KV_COLS), jnp.bfloat16),
            jax.ShapeDtypeStruct((rows, KV_COLS), jnp.float32),
        ] + cast_shapes,
        compiler_params=_params(("arbitrary",)),
        name="proj",
    )(x, g, w_in, qk_gain, *tables, *cast_args)


def _memkv_kernel(x_ref, g_ref, w_ref, qkg_ref, of_ref, ob_ref):
    xn = _rms_rows(x_ref[...], g_ref[...]).astype(jnp.bfloat16)
    gw = H_M * HEAD_DIM
    for part in range(2):
        p = jnp.dot(xn, w_ref[:, part * gw:(part + 1) * gw], preferred_element_type=jnp.float32)
        for h in range(H_M):
            z = p[:, h * HEAD_DIM:(h + 1) * HEAD_DIM]
            if part == 0:
                z = _rms_rows(z, qkg_ref[5:6, :])
            cols = slice(part * gw + h * HEAD_DIM, part * gw + (h + 1) * HEAD_DIM)
            of_ref[:, cols] = z
            ob_ref[:, cols] = z.astype(ob_ref.dtype)


def _memkv(mem, g, w_kv, qk_gain, layer):
    rows, d = mem.shape
    tm = _row_tile(rows, ROW_TILE)
    return pl.pallas_call(
        _memkv_kernel,
        grid=(rows // tm,),
        in_specs=[
            pl.BlockSpec((tm, d), lambda i: (i, 0)),
            _layer_spec((1, d), lambda i: (layer, 0, 0)),
            _layer_spec((d, MEM_COLS), lambda i: (layer, 0, 0)),
            _layer_spec(qk_gain.shape[1:], lambda i: (layer, 0, 0)),
        ],
        out_specs=[pl.BlockSpec((tm, MEM_COLS), lambda i: (i, 0))] * 2,
        out_shape=[jax.ShapeDtypeStruct((rows, MEM_COLS), jnp.float32),
                   jax.ShapeDtypeStruct((rows, MEM_COLS), jnp.bfloat16)],
        compiler_params=_params(("parallel",)),
        name="memkv",
    )(mem, g, w_kv, qk_gain)


def _attn_kernel(sink_ref, q_ref, kv_ref, mkv_ref, ba_ref, bb_ref, x_ref, w_ref, y_ref, o_ref, *,
                 layer, q_tile, base, sub_a, pieces_b):
    tile = pl.program_id(1)
    row0 = pl.multiple_of(tile * q_tile, q_tile)
    n_sub = q_tile // sub_a
    n_var_a = ba_ref.shape[0]
    group = H_A // H_A_KV

    def piece_rows(off, size):
        start = base + row0 + off
        if base + off < 0:
            start = jnp.maximum(start, 0)
        return pl.ds(pl.multiple_of(start, math.gcd(q_tile, abs(off), base)), size)

    def window_a(u):
        start = base + row0 + u * sub_a - WINDOW
        if base + u * sub_a - WINDOW < 0:
            start = jnp.maximum(start, 0)
        return pl.ds(pl.multiple_of(start, math.gcd(sub_a, base)), WINDOW + sub_a)

    def bias_a(u):
        return ba_ref[jnp.minimum(tile * n_sub + u, n_var_a - 1)]

    jobs = []
    for g in range(H_A_KV):
        heads = range(g * group, (g + 1) * group)
        sinks = [sink_ref[layer, h] * LOG2E for h in heads]
        for u in range(n_sub):
            jobs.append((slice(u * sub_a, (u + 1) * sub_a), [h * HEAD_DIM for h in heads], kv_ref,
                         [window_a(u)], KA_OFF + g * HEAD_DIM, VA_OFF + g * HEAD_DIM,
                         functools.partial(bias_a, u), sinks))
    rows_b = [piece_rows(off, size) for off, size in pieces_b]
    for h in range(H_B):
        jobs.append((slice(0, q_tile), [QB_OFF + h * HEAD_DIM], kv_ref, rows_b, KB_OFF + h * HEAD_DIM,
                     VB_OFF + h * HEAD_DIM, functools.partial(lambda hh: bb_ref[0, hh], h), None))
    rows_m = [pl.ds(0, mkv_ref.shape[1])]
    for h in range(H_M):
        jobs.append((slice(0, q_tile), [QM_OFF + h * HEAD_DIM], mkv_ref, rows_m, h * HEAD_DIM,
                     (H_M + h) * HEAD_DIM, None, None))

    scores = []
    for q_rows, q_cols, k_ref, rows_list, k_col, _, _, _ in jobs:
        qs = [q_ref[0, q_rows, c:c + HEAD_DIM] for c in q_cols]
        q = qs[0] if len(qs) == 1 else jnp.concatenate(qs, axis=0)
        parts = [lax.dot_general(q, k_ref[0, rows, k_col:k_col + HEAD_DIM], _NT_DIMS,
                                 preferred_element_type=jnp.float32) for rows in rows_list]
        scores.append(parts[0] if len(parts) == 1 else jnp.concatenate(parts, axis=1))

    probs = []
    dens = []
    for (q_rows, q_cols, _, _, _, _, bias, sinks), s_all in zip(jobs, scores):
        n_rows = q_rows.stop - q_rows.start
        bias_val = None if bias is None else bias()
        es = []
        ds = []
        for i in range(len(q_cols)):
            s = s_all[i * n_rows:(i + 1) * n_rows]
            if bias_val is not None:
                s = s + bias_val
            m = jnp.max(s, axis=-1, keepdims=True)
            if sinks is not None:
                m = jnp.maximum(m, sinks[i])
            e = jnp.exp2(s - m)
            den = jnp.sum(e, axis=-1, keepdims=True)
            if sinks is not None:
                den = den + jnp.exp2(sinks[i] - m)
            es.append(e.astype(jnp.bfloat16))
            ds.append(den)
        probs.append(es[0] if len(es) == 1 else jnp.concatenate(es, axis=0))
        dens.append(ds)

    def weighted_values(job, e, ds):
        q_rows, q_cols, k_ref, rows_list, _, v_col, _, _ = job
        n_rows = q_rows.stop - q_rows.start
        acc = None
        lo = 0
        for rows in rows_list:
            pv = jnp.dot(e[:, lo:lo + rows.size], k_ref[0, rows, v_col:v_col + HEAD_DIM],
                         preferred_element_type=jnp.float32)
            acc = pv if acc is None else acc + pv
            lo += rows.size
        for i, c in enumerate(q_cols):
            o = acc[i * n_rows:(i + 1) * n_rows] / ds[i]
            o_ref[q_rows, c:c + HEAD_DIM] = o.astype(o_ref.dtype)

    for job, e, ds in zip(jobs, probs, dens):
        weighted_values(job, e, ds)

    y = x_ref[0]
    for c in range(0, Q_COLS, V7X_MXU_DIM):
        y = y + jnp.dot(o_ref[:, c:c + V7X_MXU_DIM], w_ref[c:c + V7X_MXU_DIM, :],
                        preferred_element_type=jnp.float32)
    y_ref[0] = y


def _window_pieces(past_rows, q_tile, base):
    size = past_rows if base >= past_rows else math.gcd(past_rows, q_tile)
    pieces = [(-past_rows + k * size, size) for k in range(past_rows // size)]
    return pieces + [(0, q_tile)]


def _leading_variants(masks):
    n_var = len(masks)
    while n_var > 1 and np.array_equal(masks[n_var - 2], masks[-1]):
        n_var -= 1
    return np.stack(masks[:n_var])


def _band_masks(q_tile, past_rows, n_past_chunks, n_tiles, first_real, new_rows):
    win = past_rows + q_tile
    r = np.arange(q_tile)[:, None] // CHUNK
    w = np.arange(win)[None, :]
    band = (w // CHUNK >= r) & (w // CHUNK <= r + n_past_chunks)
    masks = []
    for i in range(n_tiles):
        row = i * q_tile + w - past_rows
        masks.append(band & (row >= first_real) & (row < new_rows))
    return _leading_variants(masks)


def _window_masks_a(sub, n_sub_tiles, base, first_real, new_rows):
    masks = []
    for s in range(n_sub_tiles):
        q_row = s * sub + np.arange(sub)[:, None]
        k_row = max(s * sub - WINDOW, -base) + np.arange(WINDOW + sub)[None, :]
        q_chunk = q_row // CHUNK
        k_chunk = np.floor_divide(k_row, CHUNK)
        masks.append((k_chunk >= q_chunk - WIN_CHUNKS) & (k_chunk <= q_chunk)
                     & (k_row >= first_real) & (k_row < new_rows))
    return _leading_variants(masks)


def _rel_bias_tiles(rel_tab, q_tile):
    win = BAND_ROWS + q_tile
    n = win + q_tile
    k = np.arange(n)
    dist = np.where(k < win, BAND_ROWS - k, BAND_ROWS + n - k)
    seq = rel_tab[:, :, np.clip(dist, -REL_CLIP, REL_CLIP) + REL_CLIP].astype(jnp.float32) * LOG2E
    flat = jnp.tile(seq, (1, 1, q_tile))[:, :, :q_tile * (n - 1)]
    return flat.reshape(rel_tab.shape[0], H_B, q_tile, n - 1)[..., :win]


def _attn_biases(rel_tab, q_tile, n_tiles, base, past_a, past_b, new_rows):
    sub = min(q_tile, WINDOW)
    mask_a = _window_masks_a(sub, n_tiles * (q_tile // sub), base, -past_a, new_rows)
    mask_b = _band_masks(q_tile, BAND_ROWS, BAND_CHUNKS, n_tiles, -past_b, new_rows)
    bias_a = jnp.asarray(np.where(mask_a, 0.0, MASK_VALUE), jnp.float32)
    rel = _rel_bias_tiles(rel_tab, q_tile)
    bias_b = jnp.where(jnp.asarray(mask_b)[None, :, None], rel[:, None], MASK_VALUE)
    return bias_a, bias_b


def _mixer(x, q, kv, mkv, sinks, bias_a, bias_b, w_out, q_tile, base, layer):
    b, tq, d = x.shape
    assert kv.shape[1] == base + tq and q.shape[1] == tq
    n_b = bias_b.shape[1]
    win_b = BAND_ROWS + q_tile
    kern = functools.partial(_attn_kernel, layer=layer, q_tile=q_tile, base=base,
                             sub_a=bias_a.shape[1],
                             pieces_b=_window_pieces(BAND_ROWS, q_tile, base))
    return pl.pallas_call(
        kern,
        grid=(b, tq // q_tile),
        in_specs=[
            pl.BlockSpec(memory_space=pltpu.SMEM),
            pl.BlockSpec((1, q_tile, Q_COLS), lambda bi, i: (bi, i, 0)),
            pl.BlockSpec((1,) + kv.shape[1:], lambda bi, i: (bi, 0, 0)),
            pl.BlockSpec((1,) + mkv.shape[1:], lambda bi, i: (bi, 0, 0)),
            pl.BlockSpec(bias_a.shape, lambda bi, i: (0, 0, 0)),
            _layer_spec((1, H_B, q_tile, win_b),
                        lambda bi, i: (layer, jnp.minimum(i, n_b - 1), 0, 0, 0)),
            pl.BlockSpec((1, q_tile, d), lambda bi, i: (bi, i, 0)),
            _layer_spec((Q_COLS, d), lambda bi, i: (layer, 0, 0), pipeline_mode=pl.Buffered(1)),
        ],
        out_specs=pl.BlockSpec((1, q_tile, d), lambda bi, i: (bi, i, 0)),
        out_shape=jax.ShapeDtypeStruct((b, tq, d), jnp.float32),
        scratch_shapes=[pltpu.VMEM((q_tile, Q_COLS), jnp.bfloat16)],
        compiler_params=_params(("parallel", "arbitrary")),
        name="mixer",
    )(sinks, q, kv, mkv, bias_a, bias_b, x, w_out)


def kernel(x_prompt, x_sample, cache_a_k, cache_a_v, cache_b_k, cache_b_v, cache_mem_k, cache_mem_v,
           mem_prompt, norm_ff1, w_ff1_in, w_ff1_out, norm_mix, w_in, qk_gain, attn_sinks, rel_bias,
           norm_mem, w_mem_kv, w_out, norm_ff2, w_ff2_in, w_ff2_out):
    b_p, t_p, d = x_prompt.shape
    b_s, t_s, _ = x_sample.shape
    depth = w_in.shape[0]
    n_mem = mem_prompt.shape[1]
    bf16 = jnp.bfloat16
    assert t_p % PROMPT_Q_TILE == 0 and t_s <= CHUNK

    ride = _ffn_cast_fits(b_p * t_p, d, w_ff1_out.shape[1])
    ff1_w = _ffn_weights_bf16(w_ff1_in, w_ff1_out, 0)
    w_in_b = w_in.astype(bf16)
    w_out_b = w_out.astype(bf16)
    w_mem_b = w_mem_kv.astype(bf16)
    g_ff1 = norm_ff1[:, None, :]
    g_ff2 = norm_ff2[:, None, :]
    g_mix = norm_mix[:, None, :]
    g_mem = norm_mem[:, None, :]

    tab_p = _rope_tables(jnp.arange(t_p, dtype=jnp.int32))
    tab_s = _rope_tables(PAST_LEN + jnp.arange(t_s, dtype=jnp.int32))
    keep_a = min(WINDOW, t_p)
    keep_b = min(BAND_ROWS, t_p)
    la = cache_a_k.shape[2]
    lb = cache_b_k.shape[2]

    bias_a_p, bias_b_p = _attn_biases(rel_bias, PROMPT_Q_TILE, t_p // PROMPT_Q_TILE, 0, 0, 0, t_p)
    bias_a_s, bias_b_s = _attn_biases(rel_bias, CHUNK, 1, BAND_ROWS, la, lb, t_s)

    def heads_flat(c):
        return c.reshape(c.shape[:3] + (-1,)).astype(bf16)

    def front(c, rows):
        return jnp.pad(c, ((0, 0), (0, 0), (rows - c.shape[2], 0), (0, 0)))

    past_kv = jnp.concatenate(
        [front(heads_flat(cache_a_k), BAND_ROWS), front(heads_flat(cache_a_v), BAND_ROWS),
         front(heads_flat(cache_b_k), BAND_ROWS), front(heads_flat(cache_b_v), BAND_ROWS)], axis=3)
    mkv_s = jnp.concatenate([heads_flat(cache_mem_k), heads_flat(cache_mem_v)], axis=3)

    xp = x_prompt.reshape(b_p * t_p, d)
    xs = x_sample.reshape(b_s * t_s, d)
    mem_flat = mem_prompt.reshape(b_p * n_mem, d)
    outs = [[] for _ in range(10)]
    for l in range(depth):
        mkv_f, mkv_b = _memkv(mem_flat, g_mem, w_mem_b, qk_gain, l)
        xp = _ffn(xp, g_ff1, ff1_w, l)
        casts = [(w_ff2_in, w_ff2_out, l)]
        if l + 1 < depth:
            casts.append((w_ff1_in, w_ff1_out, l + 1))
        q, kvb, kvf, *cast_w = _proj(xp, g_mix, w_in_b, qk_gain, tab_p, t_p, l, casts if ride else ())
        if ride:
            ff2_w, next_ff1_w = cast_w[:3], cast_w[3:]
        else:
            ff2_w = _ffn_weights_bf16(w_ff2_in, w_ff2_out, l)
            next_ff1_w = _ffn_weights_bf16(w_ff1_in, w_ff1_out, l + 1) if l + 1 < depth else None
        xp = _mixer(xp.reshape(b_p, t_p, d), q.reshape(b_p, t_p, Q_COLS),
                    kvb.reshape(b_p, t_p, KV_COLS), mkv_b.reshape(b_p, n_mem, MEM_COLS),
                    attn_sinks, bias_a_p, bias_b_p, w_out_b, PROMPT_Q_TILE, 0, l).reshape(b_p * t_p, d)
        xp = _ffn(xp, g_ff2, ff2_w, l)
        kvf = kvf.reshape(b_p, t_p, KV_COLS)
        outs[0].append(kvf[:, t_p - keep_a:, KA_OFF:VA_OFF])
        outs[1].append(kvf[:, t_p - keep_a:, VA_OFF:KB_OFF])
        outs[2].append(kvf[:, t_p - keep_b:, KB_OFF:VB_OFF])
        outs[3].append(kvf[:, t_p - keep_b:, VB_OFF:])
        outs[4].append(mkv_f[:, :H_M * HEAD_DIM])
        outs[5].append(mkv_f[:, H_M * HEAD_DIM:])

        xs = _ffn(xs, g_ff1, ff1_w, l)
        q, kvb, kvf = _proj(xs, g_mix, w_in_b, qk_gain, tab_s, t_s, l)
        tail = ((0, 0), (0, CHUNK - t_s), (0, 0))
        kv = jnp.concatenate([past_kv[l], jnp.pad(kvb.reshape(b_s, t_s, KV_COLS), tail)], axis=1)
        y = _mixer(jnp.pad(xs.reshape(b_s, t_s, d), tail), jnp.pad(q.reshape(b_s, t_s, Q_COLS), tail),
                   kv, mkv_s[l], attn_sinks, bias_a_s, bias_b_s, w_out_b, CHUNK, BAND_ROWS, l)
        xs = _ffn(y[:, :t_s].reshape(b_s * t_s, d), g_ff2, ff2_w, l)
        outs[6].append(kvf[:, KA_OFF:VA_OFF])
        outs[7].append(kvf[:, VA_OFF:KB_OFF])
        outs[8].append(kvf[:, KB_OFF:VB_OFF])
        outs[9].append(kvf[:, VB_OFF:])
        ff1_w = next_ff1_w

    def stacked_heads(per_layer, batch):
        z = jnp.stack(per_layer)
        return z.reshape(depth, batch, -1, z.shape[-1] // HEAD_DIM, HEAD_DIM)

    return ((xp.reshape(b_p, t_p, d), xs.reshape(b_s, t_s, d))
            + tuple(stacked_heads(o, b_p) for o in outs[:6])
            + tuple(stacked_heads(o, b_s) for o in outs[6:]))
```

```python
import functools
import math

import numpy as np
import jax
import jax.numpy as jnp
from jax import lax
from jax.experimental import pallas as pl
from jax.experimental.pallas import tpu as pltpu

CHUNK = 64
HEAD_DIM = 128
H_A = 8
H_A_KV = 2
WIN_CHUNKS = 2
WINDOW = WIN_CHUNKS * CHUNK
H_B = 4
BAND_CHUNKS = 8
BAND_ROWS = BAND_CHUNKS * CHUNK
REL_CLIP = 128
H_M = 4
ROT_DIM = HEAD_DIM // 4
ROPE_THETA = 500000.0
EPS = 1e-6
PAST_LEN = 1024
LOG2E = math.log2(math.e)
Q_SCALE = HEAD_DIM ** -0.5 * LOG2E

Q_COLS = (H_A + H_B + H_M) * HEAD_DIM
KV_COLS = (2 * H_A_KV + 2 * H_B) * HEAD_DIM
MEM_COLS = 2 * H_M * HEAD_DIM
QB_OFF = H_A * HEAD_DIM
QM_OFF = (H_A + H_B) * HEAD_DIM
KA_OFF = 0
VA_OFF = H_A_KV * HEAD_DIM
KB_OFF = 2 * H_A_KV * HEAD_DIM
VB_OFF = KB_OFF + H_B * HEAD_DIM

V7X_VMEM_LIMIT_BYTES = 62 * 1024 * 1024
HEAD_GROUP = 4
FF_TILE = 1024
FFN_ROW_TILE = 1024
V7X_LANES = 128
V7X_BF16_SUBLANES = 16
V7X_MXU_DIM = 256
CAST_ROW_TILE = 256
ROW_TILE = 512
PROMPT_Q_TILE = 256
MASK_VALUE = -1e30

_NT_DIMS = (((1,), (1,)), ((), ()))


def _row_tile(rows, preferred):
    tile = min(rows, preferred)
    while rows % tile:
        tile //= 2
    return tile


def _params(semantics):
    return pltpu.CompilerParams(dimension_semantics=semantics,
                                vmem_limit_bytes=V7X_VMEM_LIMIT_BYTES)


def _rms_rows(x, g):
    ms = jnp.mean(x * x, axis=-1, keepdims=True)
    return x * lax.rsqrt(ms + EPS) * g


def _layer_spec(shape, index_map, **kwargs):
    return pl.BlockSpec((None,) + tuple(shape), index_map, **kwargs)


def _ragged_step(n_ff, last_cols):
    return 1 if (last_cols != FF_TILE and n_ff > 2) else n_ff - 1


def _ff_block(j, n_ff, last_cols):
    ragged = _ragged_step(n_ff, last_cols)
    if ragged == n_ff - 1:
        return j
    return jnp.where(j == ragged, n_ff - 1, jnp.where(j < ragged, j, j - 1))


def _ffn_kernel(x_hbm, g_ref, wg_ref, wu_ref, wo_ref, o_ref, xn_ref, x_ref, x_sem, *, n_ff, last_cols):
    i = pl.program_id(0)
    j = pl.program_id(1)
    n_tiles = pl.num_programs(0)
    tm = x_ref.shape[0]
    ragged = _ragged_step(n_ff, last_cols)

    def x_copy(tile):
        rows = pl.ds(pl.multiple_of(tile * tm, tm), tm)
        return pltpu.make_async_copy(x_hbm.at[rows, :], x_ref, x_sem)

    @pl.when((j == 0) if n_ff == 1 else (i == 0) & (j == 0))
    def _():
        x_copy(i).start()

    @pl.when(j == 0)
    def _():
        x_copy(i).wait()

    if n_ff > 1:
        @pl.when((j == 1) & (i + 1 < n_tiles))
        def _():
            x_copy(i + 1).start()

    def accumulate_onto(base_ref, cols):
        xn = xn_ref[...]
        gate = jnp.dot(xn, wg_ref[:, :cols], preferred_element_type=jnp.float32)
        up = jnp.dot(xn, wu_ref[:, :cols], preferred_element_type=jnp.float32)
        act = (gate * jax.nn.sigmoid(gate) * up * 0.5).astype(jnp.bfloat16)
        o_ref[...] = base_ref[...] + jnp.dot(act, wo_ref[:cols, :], preferred_element_type=jnp.float32)

    def normalise():
        xn_ref[...] = _rms_rows(x_ref[...], g_ref[...]).astype(xn_ref.dtype)

    @pl.when(j == 0)
    def _():
        normalise()
        accumulate_onto(x_ref, FF_TILE if n_ff > 1 else last_cols)

    if last_cols == FF_TILE:
        @pl.when(j != 0)
        def _():
            accumulate_onto(o_ref, FF_TILE)
    elif n_ff > 1:
        @pl.when((j != 0) & (j != ragged))
        def _():
            accumulate_onto(o_ref, FF_TILE)

        @pl.when(j == ragged)
        def _():
            accumulate_onto(o_ref, last_cols)


def _ffn(x, g, weights, layer):
    wg, wu, wo = weights
    rows, d = x.shape
    f = wo.shape[0]
    n_ff = pl.cdiv(f, FF_TILE)
    tm = _row_tile(rows, FFN_ROW_TILE)
    last_cols = f - (n_ff - 1) * FF_TILE
    block = functools.partial(_ff_block, n_ff=n_ff, last_cols=last_cols)
    return pl.pallas_call(
        functools.partial(_ffn_kernel, n_ff=n_ff, last_cols=last_cols),
        grid=(rows // tm, n_ff),
        in_specs=[
            pl.BlockSpec(memory_space=pl.ANY),
            _layer_spec((1, d), lambda i, j: (layer, 0, 0)),
            pl.BlockSpec((d, FF_TILE), lambda i, j: (0, block(j))),
            pl.BlockSpec((d, FF_TILE), lambda i, j: (0, block(j))),
            pl.BlockSpec((FF_TILE, d), lambda i, j: (block(j), 0)),
        ],
        out_specs=pl.BlockSpec((tm, d), lambda i, j: (i, 0)),
        out_shape=jax.ShapeDtypeStruct((rows, d), jnp.float32),
        scratch_shapes=[pltpu.VMEM((tm, d), jnp.bfloat16), pltpu.VMEM((tm, d), jnp.float32),
                        pltpu.SemaphoreType.DMA(())],
        compiler_params=_params(("arbitrary", "arbitrary")),
        name="ffn",
    )(x, g, wg, wu, wo)


def _cast_kernel(w_ref, o_ref):
    o_ref[...] = w_ref[...].astype(o_ref.dtype)


def _ffn_weights_bf16(w_i, w_o, layer):
    _, d, two_f = w_i.shape
    f = two_f // 2
    assert f % V7X_LANES == 0
    rt = _row_tile(d, CAST_ROW_TILE)

    def half(part):
        return pl.pallas_call(
            _cast_kernel,
            grid=(d // rt,),
            in_specs=[_layer_spec((rt, f), lambda r: (layer, r, part))],
            out_specs=pl.BlockSpec((rt, f), lambda r: (r, 0)),
            out_shape=jax.ShapeDtypeStruct((d, f), jnp.bfloat16),
            compiler_params=_params(("parallel",)),
            name="cast_cols",
        )(w_i)

    return half(0), half(1), w_o[layer].astype(jnp.bfloat16)


def _head_plan():
    plan = []
    for h in range(H_A):
        plan.append((0, True, True, h * HEAD_DIM))
    for h in range(H_A_KV):
        plan.append((1, True, False, KA_OFF + h * HEAD_DIM))
    for h in range(H_A_KV):
        plan.append((None, False, False, VA_OFF + h * HEAD_DIM))
    for h in range(H_B):
        plan.append((2, False, True, QB_OFF + h * HEAD_DIM))
    for h in range(H_B):
        plan.append((3, False, False, KB_OFF + h * HEAD_DIM))
    for h in range(H_B):
        plan.append((None, False, False, VB_OFF + h * HEAD_DIM))
    for h in range(H_M):
        plan.append((4, False, True, QM_OFF + h * HEAD_DIM))
    return plan


_PLAN = _head_plan()
IN_COLS = len(_PLAN) * HEAD_DIM


def _proj_kernel(x_ref, g_ref, w_ref, qkg_ref, cos_ref, sa_ref, sb_ref, *rest, n_cast):
    cast_in = rest[:n_cast]
    q_ref, kvb_ref, kvf_ref = rest[n_cast:n_cast + 3]
    cast_out = rest[n_cast + 3:]
    for src, dst in zip(cast_in, cast_out):
        dst[...] = src[...].astype(dst.dtype)

    xn = _rms_rows(x_ref[...], g_ref[...]).astype(jnp.bfloat16)
    cos = cos_ref[...]
    sa = sa_ref[...]
    sb = sb_ref[...]
    gw = HEAD_GROUP * HEAD_DIM
    groups = sorted(range(len(_PLAN) // HEAD_GROUP),
                    key=lambda g: all(_PLAN[g * HEAD_GROUP + hh][0] is None for hh in range(HEAD_GROUP)))
    for grp in groups:
        p = jnp.dot(xn, w_ref[:, grp * gw:(grp + 1) * gw], preferred_element_type=jnp.float32)
        for hh in range(HEAD_GROUP):
            gain, rotary, is_query, col = _PLAN[grp * HEAD_GROUP + hh]
            z = p[:, hh * HEAD_DIM:(hh + 1) * HEAD_DIM]
            if gain is not None:
                z = _rms_rows(z, qkg_ref[gain:gain + 1, :])
            if rotary:
                z = (z * cos + pltpu.roll(z, HEAD_DIM - ROT_DIM // 2, 1) * sa
                     + pltpu.roll(z, ROT_DIM // 2, 1) * sb)
            cols = slice(col, col + HEAD_DIM)
            if is_query:
                q_ref[:, cols] = (z * Q_SCALE).astype(q_ref.dtype)
            else:
                kvf_ref[:, cols] = z
                kvb_ref[:, cols] = z.astype(kvb_ref.dtype)


def _rope_tables(pos):
    half = ROT_DIM // 2
    inv_freq = ROPE_THETA ** (-jnp.arange(half, dtype=jnp.float32) / half)
    ang = pos.astype(jnp.float32)[:, None] * inv_freq[None, :]
    cos = jnp.cos(ang)
    sin = jnp.sin(ang)
    t = pos.shape[0]
    ones = jnp.ones((t, HEAD_DIM - ROT_DIM), jnp.float32)
    cos_t = jnp.concatenate([cos, cos, ones], axis=1)
    sa_t = jnp.concatenate([-sin, jnp.zeros((t, HEAD_DIM - half), jnp.float32)], axis=1)
    sb_t = jnp.concatenate([jnp.zeros((t, half), jnp.float32), sin,
                            jnp.zeros((t, HEAD_DIM - ROT_DIM), jnp.float32)], axis=1)
    return cos_t, sa_t, sb_t


def _ffn_cast_fits(rows, d, f):
    n_steps = rows // _row_tile(rows, ROW_TILE)
    return (all(r % n_steps == 0 and (r // n_steps) % V7X_BF16_SUBLANES == 0 for r in (d, Q_COLS))
            and f % V7X_LANES == 0 and n_steps >= f // V7X_LANES)


def _ffn_cast_streams(w_i, w_o, layer, n_steps):
    _, d, two_f = w_i.shape
    f = two_f // 2
    rd = d // n_steps
    last = f // V7X_LANES - 1
    in_specs = [
        _layer_spec((rd, f), lambda i: (layer, i, 0)),
        _layer_spec((rd, f), lambda i: (layer, i, 1)),
        _layer_spec((V7X_LANES, d), lambda i: (layer, jnp.minimum(i, last), 0)),
    ]
    out_specs = [
        pl.BlockSpec((rd, f), lambda i: (i, 0)),
        pl.BlockSpec((rd, f), lambda i: (i, 0)),
        pl.BlockSpec((V7X_LANES, d), lambda i: (jnp.minimum(i, last), 0)),
    ]
    out_shapes = [jax.ShapeDtypeStruct((d, f), jnp.bfloat16)] * 2 + [
        jax.ShapeDtypeStruct((f, d), jnp.bfloat16)]
    return [w_i, w_i, w_o], in_specs, out_specs, out_shapes


def _row_cast_stream(w, layer, n_steps):
    _, r, c = w.shape
    rr = r // n_steps
    assert r % n_steps == 0 and rr % V7X_BF16_SUBLANES == 0
    return ([w], [_layer_spec((rr, c), lambda i: (layer, i, 0))],
            [pl.BlockSpec((rr, c), lambda i: (i, 0))], [jax.ShapeDtypeStruct((r, c), jnp.bfloat16)])


def _proj(x, g, w_in, qk_gain, tables, seq_len, layer, casts=(), row_casts=()):
    rows, d = x.shape
    tm = _row_tile(rows, ROW_TILE)
    cast_args, cast_in, cast_out, cast_shapes = [], [], [], []
    streams = [_ffn_cast_streams(w_i, w_o, cast_layer, rows // tm) for w_i, w_o, cast_layer in casts]
    streams += [_row_cast_stream(w, cast_layer, rows // tm) for w, cast_layer in row_casts]
    for a, i_s, o_s, shp in streams:
        cast_args += a
        cast_in += i_s
        cast_out += o_s
        cast_shapes += shp
    if tm <= seq_len:
        assert seq_len % tm == 0
        per_seq = seq_len // tm
        tab_map = lambda i: (i % per_seq, 0)
    else:
        assert tm % seq_len == 0
        tables = tuple(jnp.tile(t, (tm // seq_len, 1)) for t in tables)
        tab_map = lambda i: (0, 0)
    tab_spec = pl.BlockSpec((tm, HEAD_DIM), tab_map)
    return pl.pallas_call(
        functools.partial(_proj_kernel, n_cast=len(cast_args)),
        grid=(rows // tm,),
        in_specs=[
            pl.BlockSpec((tm, d), lambda i: (i, 0)),
            _layer_spec((1, d), lambda i: (layer, 0, 0)),
            pl.BlockSpec((d, IN_COLS), lambda i: (0, 0), pipeline_mode=pl.Buffered(1)),
            _layer_spec(qk_gain.shape[1:], lambda i: (layer, 0, 0)),
            tab_spec, tab_spec, tab_spec,
        ] + cast_in,
        out_specs=[
            pl.BlockSpec((tm, Q_COLS), lambda i: (i, 0)),
            pl.BlockSpec((tm, KV_COLS), lambda i: (i, 0)),
            pl.BlockSpec((tm, KV_COLS), lambda i: (i, 0)),
        ] + cast_out,
        out_shape=[
            jax.ShapeDtypeStruct((rows, Q_COLS), jnp.bfloat16),
            jax.ShapeDtypeStruct((rows, KV_COLS), jnp.bfloat16),
            jax.ShapeDtypeStruct((rows, KV_COLS), jnp.float32),
        ] + cast_shapes,
        compiler_params=_params(("arbitrary",)),
        name="proj",
    )(x, g, w_in, qk_gain, *tables, *cast_args)


def _memkv_kernel(x_ref, g_ref, w_ref, qkg_ref, of_ref, ob_ref):
    xn = _rms_rows(x_ref[...], g_ref[...]).astype(jnp.bfloat16)
    gw = H_M * HEAD_DIM
    for part in range(2):
        p = jnp.dot(xn, w_ref[:, part * gw:(part + 1) * gw], preferred_element_type=jnp.float32)
        for h in range(H_M):
            z = p[:, h * HEAD_DIM:(h + 1) * HEAD_DIM]
            if part == 0:
                z = _rms_rows(z, qkg_ref[5:6, :])
            cols = slice(part * gw + h * HEAD_DIM, part * gw + (h + 1) * HEAD_DIM)
            of_ref[:, cols] = z
            ob_ref[:, cols] = z.astype(ob_ref.dtype)


def _memkv(mem, g, w_kv, qk_gain, layer):
    rows, d = mem.shape
    tm = _row_tile(rows, ROW_TILE)
    return pl.pallas_call(
        _memkv_kernel,
        grid=(rows // tm,),
        in_specs=[
            pl.BlockSpec((tm, d), lambda i: (i, 0)),
            _layer_spec((1, d), lambda i: (layer, 0, 0)),
            pl.BlockSpec((d, MEM_COLS), lambda i: (0, 0)),
            _layer_spec(qk_gain.shape[1:], lambda i: (layer, 0, 0)),
        ],
        out_specs=[pl.BlockSpec((tm, MEM_COLS), lambda i: (i, 0))] * 2,
        out_shape=[jax.ShapeDtypeStruct((rows, MEM_COLS), jnp.float32),
                   jax.ShapeDtypeStruct((rows, MEM_COLS), jnp.bfloat16)],
        compiler_params=_params(("parallel",)),
        name="memkv",
    )(mem, g, w_kv, qk_gain)


def _attn_kernel(sink_ref, q_ref, kv_ref, mkv_ref, ba_ref, bb_ref, x_ref, w_ref, y_ref, o_ref, *,
                 layer, q_tile, base, sub_a, pieces_b):
    tile = pl.program_id(1)
    row0 = pl.multiple_of(tile * q_tile, q_tile)
    n_sub = q_tile // sub_a
    n_var_a = ba_ref.shape[0]
    group = H_A // H_A_KV

    def piece_rows(off, size):
        start = base + row0 + off
        if base + off < 0:
            start = jnp.maximum(start, 0)
        return pl.ds(pl.multiple_of(start, math.gcd(q_tile, abs(off), base)), size)

    def window_a(u):
        start = base + row0 + u * sub_a - WINDOW
        if base + u * sub_a - WINDOW < 0:
            start = jnp.maximum(start, 0)
        return pl.ds(pl.multiple_of(start, math.gcd(sub_a, base)), WINDOW + sub_a)

    def bias_a(u):
        return ba_ref[jnp.minimum(tile * n_sub + u, n_var_a - 1)]

    jobs = []
    for g in range(H_A_KV):
        heads = range(g * group, (g + 1) * group)
        sinks = [sink_ref[layer, h] * LOG2E for h in heads]
        for u in range(n_sub):
            jobs.append((slice(u * sub_a, (u + 1) * sub_a), [h * HEAD_DIM for h in heads], kv_ref,
                         [window_a(u)], KA_OFF + g * HEAD_DIM, VA_OFF + g * HEAD_DIM,
                         functools.partial(bias_a, u), sinks))
    rows_b = [piece_rows(off, size) for off, size in pieces_b]
    for h in range(H_B):
        jobs.append((slice(0, q_tile), [QB_OFF + h * HEAD_DIM], kv_ref, rows_b, KB_OFF + h * HEAD_DIM,
                     VB_OFF + h * HEAD_DIM, functools.partial(lambda hh: bb_ref[0, hh], h), None))
    rows_m = [pl.ds(0, mkv_ref.shape[1])]
    for h in range(H_M):
        jobs.append((slice(0, q_tile), [QM_OFF + h * HEAD_DIM], mkv_ref, rows_m, h * HEAD_DIM,
                     (H_M + h) * HEAD_DIM, None, None))

    scores = []
    for q_rows, q_cols, k_ref, rows_list, k_col, _, _, _ in jobs:
        qs = [q_ref[0, q_rows, c:c + HEAD_DIM] for c in q_cols]
        q = qs[0] if len(qs) == 1 else jnp.concatenate(qs, axis=0)
        parts = [lax.dot_general(q, k_ref[0, rows, k_col:k_col + HEAD_DIM], _NT_DIMS,
                                 preferred_element_type=jnp.float32) for rows in rows_list]
        scores.append(parts[0] if len(parts) == 1 else jnp.concatenate(parts, axis=1))

    probs = []
    dens = []
    for (q_rows, q_cols, _, _, _, _, bias, sinks), s_all in zip(jobs, scores):
        n_rows = q_rows.stop - q_rows.start
        bias_val = None if bias is None else bias()
        es = []
        ds = []
        for i in range(len(q_cols)):
            s = s_all[i * n_rows:(i + 1) * n_rows]
            if bias_val is not None:
                s = s + bias_val
            m = jnp.max(s, axis=-1, keepdims=True)
            if sinks is not None:
                m = jnp.maximum(m, sinks[i])
            e = jnp.exp2(s - m)
            den = jnp.sum(e, axis=-1, keepdims=True)
            if sinks is not None:
                den = den + jnp.exp2(sinks[i] - m)
            es.append(e.astype(jnp.bfloat16))
            ds.append(den)
        probs.append(es[0] if len(es) == 1 else jnp.concatenate(es, axis=0))
        dens.append(ds)

    def weighted_values(job, e, ds):
        q_rows, q_cols, k_ref, rows_list, _, v_col, _, _ = job
        n_rows = q_rows.stop - q_rows.start
        acc = None
        lo = 0
        for rows in rows_list:
            pv = jnp.dot(e[:, lo:lo + rows.size], k_ref[0, rows, v_col:v_col + HEAD_DIM],
                         preferred_element_type=jnp.float32)
            acc = pv if acc is None else acc + pv
            lo += rows.size
        for i, c in enumerate(q_cols):
            o = acc[i * n_rows:(i + 1) * n_rows] / ds[i]
            o_ref[q_rows, c:c + HEAD_DIM] = o.astype(o_ref.dtype)

    for job, e, ds in zip(jobs, probs, dens):
        weighted_values(job, e, ds)

    y = x_ref[0]
    for c in range(0, Q_COLS, V7X_MXU_DIM):
        y = y + jnp.dot(o_ref[:, c:c + V7X_MXU_DIM], w_ref[c:c + V7X_MXU_DIM, :],
                        preferred_element_type=jnp.float32)
    y_ref[0] = y


def _window_pieces(past_rows, q_tile, base):
    size = past_rows if base >= past_rows else math.gcd(past_rows, q_tile)
    pieces = [(-past_rows + k * size, size) for k in range(past_rows // size)]
    return pieces + [(0, q_tile)]


def _leading_variants(masks):
    n_var = len(masks)
    while n_var > 1 and np.array_equal(masks[n_var - 2], masks[-1]):
        n_var -= 1
    return np.stack(masks[:n_var])


def _band_masks(q_tile, past_rows, n_past_chunks, n_tiles, first_real, new_rows):
    win = past_rows + q_tile
    r = np.arange(q_tile)[:, None] // CHUNK
    w = np.arange(win)[None, :]
    band = (w // CHUNK >= r) & (w // CHUNK <= r + n_past_chunks)
    masks = []
    for i in range(n_tiles):
        row = i * q_tile + w - past_rows
        masks.append(band & (row >= first_real) & (row < new_rows))
    return _leading_variants(masks)


def _window_masks_a(sub, n_sub_tiles, base, first_real, new_rows):
    masks = []
    for s in range(n_sub_tiles):
        q_row = s * sub + np.arange(sub)[:, None]
        k_row = max(s * sub - WINDOW, -base) + np.arange(WINDOW + sub)[None, :]
        q_chunk = q_row // CHUNK
        k_chunk = np.floor_divide(k_row, CHUNK)
        masks.append((k_chunk >= q_chunk - WIN_CHUNKS) & (k_chunk <= q_chunk)
                     & (k_row >= first_real) & (k_row < new_rows))
    return _leading_variants(masks)


def _rel_bias_tiles(rel_tab, q_tile):
    win = BAND_ROWS + q_tile
    n = win + q_tile
    k = np.arange(n)
    dist = np.where(k < win, BAND_ROWS - k, BAND_ROWS + n - k)
    seq = rel_tab[:, :, np.clip(dist, -REL_CLIP, REL_CLIP) + REL_CLIP].astype(jnp.float32) * LOG2E
    flat = jnp.tile(seq, (1, 1, q_tile))[:, :, :q_tile * (n - 1)]
    return flat.reshape(rel_tab.shape[0], H_B, q_tile, n - 1)[..., :win]


def _attn_biases(rel_tab, q_tile, n_tiles, base, past_a, past_b, new_rows):
    sub = min(q_tile, WINDOW)
    mask_a = _window_masks_a(sub, n_tiles * (q_tile // sub), base, -past_a, new_rows)
    mask_b = _band_masks(q_tile, BAND_ROWS, BAND_CHUNKS, n_tiles, -past_b, new_rows)
    bias_a = jnp.asarray(np.where(mask_a, 0.0, MASK_VALUE), jnp.float32)
    rel = _rel_bias_tiles(rel_tab, q_tile)
    bias_b = jnp.where(jnp.asarray(mask_b)[None, :, None], rel[:, None], MASK_VALUE)
    return bias_a, bias_b


def _mixer(x, q, kv, mkv, sinks, bias_a, bias_b, w_out, q_tile, base, layer):
    b, tq, d = x.shape
    assert kv.shape[1] == base + tq and q.shape[1] == tq
    n_b = bias_b.shape[1]
    win_b = BAND_ROWS + q_tile
    kern = functools.partial(_attn_kernel, layer=layer, q_tile=q_tile, base=base,
                             sub_a=bias_a.shape[1],
                             pieces_b=_window_pieces(BAND_ROWS, q_tile, base))
    return pl.pallas_call(
        kern,
        grid=(b, tq // q_tile),
        in_specs=[
            pl.BlockSpec(memory_space=pltpu.SMEM),
            pl.BlockSpec((1, q_tile, Q_COLS), lambda bi, i: (bi, i, 0)),
            pl.BlockSpec((1,) + kv.shape[1:], lambda bi, i: (bi, 0, 0)),
            pl.BlockSpec((1,) + mkv.shape[1:], lambda bi, i: (bi, 0, 0)),
            pl.BlockSpec(bias_a.shape, lambda bi, i: (0, 0, 0)),
            _layer_spec((1, H_B, q_tile, win_b),
                        lambda bi, i: (layer, jnp.minimum(i, n_b - 1), 0, 0, 0)),
            pl.BlockSpec((1, q_tile, d), lambda bi, i: (bi, i, 0)),
            pl.BlockSpec((Q_COLS, d), lambda bi, i: (0, 0), pipeline_mode=pl.Buffered(1)),
        ],
        out_specs=pl.BlockSpec((1, q_tile, d), lambda bi, i: (bi, i, 0)),
        out_shape=jax.ShapeDtypeStruct((b, tq, d), jnp.float32),
        scratch_shapes=[pltpu.VMEM((q_tile, Q_COLS), jnp.bfloat16)],
        compiler_params=_params(("parallel", "arbitrary")),
        name="mixer",
    )(sinks, q, kv, mkv, bias_a, bias_b, x, w_out)


def kernel(x_prompt, x_sample, cache_a_k, cache_a_v, cache_b_k, cache_b_v, cache_mem_k, cache_mem_v,
           mem_prompt, norm_ff1, w_ff1_in, w_ff1_out, norm_mix, w_in, qk_gain, attn_sinks, rel_bias,
           norm_mem, w_mem_kv, w_out, norm_ff2, w_ff2_in, w_ff2_out):
    b_p, t_p, d = x_prompt.shape
    b_s, t_s, _ = x_sample.shape
    depth = w_in.shape[0]
    n_mem = mem_prompt.shape[1]
    bf16 = jnp.bfloat16
    assert t_p % PROMPT_Q_TILE == 0 and t_s <= CHUNK

    ride = _ffn_cast_fits(b_p * t_p, d, w_ff1_out.shape[1])
    mix_stacks = (w_in, w_out, w_mem_kv)
    ff1_w = _ffn_weights_bf16(w_ff1_in, w_ff1_out, 0)
    mix_w = [w[0].astype(bf16) for w in mix_stacks]
    g_ff1 = norm_ff1[:, None, :]
    g_ff2 = norm_ff2[:, None, :]
    g_mix = norm_mix[:, None, :]
    g_mem = norm_mem[:, None, :]

    tab_p = _rope_tables(jnp.arange(t_p, dtype=jnp.int32))
    tab_s = _rope_tables(PAST_LEN + jnp.arange(t_s, dtype=jnp.int32))
    keep_a = min(WINDOW, t_p)
    keep_b = min(BAND_ROWS, t_p)
    la = cache_a_k.shape[2]
    lb = cache_b_k.shape[2]

    bias_a_p, bias_b_p = _attn_biases(rel_bias, PROMPT_Q_TILE, t_p // PROMPT_Q_TILE, 0, 0, 0, t_p)
    bias_a_s, bias_b_s = _attn_biases(rel_bias, CHUNK, 1, BAND_ROWS, la, lb, t_s)

    def heads_flat(c):
        return c.reshape(c.shape[:3] + (-1,)).astype(bf16)

    def front(c, rows):
        return jnp.pad(c, ((0, 0), (0, 0), (rows - c.shape[2], 0), (0, 0)))

    past_kv = jnp.concatenate(
        [front(heads_flat(cache_a_k), BAND_ROWS), front(heads_flat(cache_a_v), BAND_ROWS),
         front(heads_flat(cache_b_k), BAND_ROWS), front(heads_flat(cache_b_v), BAND_ROWS)], axis=3)
    mkv_s = jnp.concatenate([heads_flat(cache_mem_k), heads_flat(cache_mem_v)], axis=3)

    xp = x_prompt.reshape(b_p * t_p, d)
    xs = x_sample.reshape(b_s * t_s, d)
    mem_flat = mem_prompt.reshape(b_p * n_mem, d)
    outs = [[] for _ in range(10)]
    for l in range(depth):
        w_in_l, w_out_l, w_mem_l = mix_w
        mkv_f, mkv_b = _memkv(mem_flat, g_mem, w_mem_l, qk_gain, l)
        xp = _ffn(xp, g_ff1, ff1_w, l)
        more = l + 1 < depth
        casts = [(w_ff2_in, w_ff2_out, l)] + ([(w_ff1_in, w_ff1_out, l + 1)] if more else [])
        row_casts = [(w, l + 1) for w in mix_stacks] if more else []
        if ride:
            q, kvb, kvf, *cast_w = _proj(xp, g_mix, w_in_l, qk_gain, tab_p, t_p, l, casts, row_casts)
            ff2_w, next_ff1_w, next_mix_w = cast_w[:3], cast_w[3:6], cast_w[6:]
        else:
            q, kvb, kvf = _proj(xp, g_mix, w_in_l, qk_gain, tab_p, t_p, l)
            ff2_w = _ffn_weights_bf16(w_ff2_in, w_ff2_out, l)
            next_ff1_w = _ffn_weights_bf16(w_ff1_in, w_ff1_out, l + 1) if more else None
            next_mix_w = [w[l + 1].astype(bf16) for w in mix_stacks] if more else None
        xp = _mixer(xp.reshape(b_p, t_p, d), q.reshape(b_p, t_p, Q_COLS),
                    kvb.reshape(b_p, t_p, KV_COLS), mkv_b.reshape(b_p, n_mem, MEM_COLS),
                    attn_sinks, bias_a_p, bias_b_p, w_out_l, PROMPT_Q_TILE, 0, l).reshape(b_p * t_p, d)
        xp = _ffn(xp, g_ff2, ff2_w, l)
        kvf = kvf.reshape(b_p, t_p, KV_COLS)
        outs[0].append(kvf[:, t_p - keep_a:, KA_OFF:VA_OFF])
        outs[1].append(kvf[:, t_p - keep_a:, VA_OFF:KB_OFF])
        outs[2].append(kvf[:, t_p - keep_b:, KB_OFF:VB_OFF])
        outs[3].append(kvf[:, t_p - keep_b:, VB_OFF:])
        outs[4].append(mkv_f[:, :H_M * HEAD_DIM])
        outs[5].append(mkv_f[:, H_M * HEAD_DIM:])

        xs = _ffn(xs, g_ff1, ff1_w, l)
        q, kvb, kvf = _proj(xs, g_mix, w_in_l, qk_gain, tab_s, t_s, l)
        tail = ((0, 0), (0, CHUNK - t_s), (0, 0))
        kv = jnp.concatenate([past_kv[l], jnp.pad(kvb.reshape(b_s, t_s, KV_COLS), tail)], axis=1)
        y = _mixer(jnp.pad(xs.reshape(b_s, t_s, d), tail), jnp.pad(q.reshape(b_s, t_s, Q_COLS), tail),
                   kv, mkv_s[l], attn_sinks, bias_a_s, bias_b_s, w_out_l, CHUNK, BAND_ROWS, l)
        xs = _ffn(y[:, :t_s].reshape(b_s * t_s, d), g_ff2, ff2_w, l)
        outs[6].append(kvf[:, KA_OFF:VA_OFF])
        outs[7].append(kvf[:, VA_OFF:KB_OFF])
        outs[8].append(kvf[:, KB_OFF:VB_OFF])
        outs[9].append(kvf[:, VB_OFF:])
        ff1_w, mix_w = next_ff1_w, next_mix_w

    def stacked_heads(per_layer, batch):
        z = jnp.stack(per_layer)
        return z.reshape(depth, batch, -1, z.shape[-1] // HEAD_DIM, HEAD_DIM)

    return ((xp.reshape(b_p, t_p, d), xs.reshape(b_s, t_s, d))
            + tuple(stacked_heads(o, b_p) for o in outs[:6])
            + tuple(stacked_heads(o, b_s) for o in outs[6:]))
```

```python
import functools
import math

import numpy as np
import jax
import jax.numpy as jnp
from jax import lax
from jax.experimental import pallas as pl
from jax.experimental.pallas import tpu as pltpu

CHUNK = 64
HEAD_DIM = 128
H_A = 8
H_A_KV = 2
WIN_CHUNKS = 2
WINDOW = WIN_CHUNKS * CHUNK
H_B = 4
BAND_CHUNKS = 8
BAND_ROWS = BAND_CHUNKS * CHUNK
REL_CLIP = 128
H_M = 4
ROT_DIM = HEAD_DIM // 4
ROPE_THETA = 500000.0
EPS = 1e-6
PAST_LEN = 1024
LOG2E = math.log2(math.e)
Q_SCALE = HEAD_DIM ** -0.5 * LOG2E

Q_COLS = (H_A + H_B + H_M) * HEAD_DIM
KV_COLS = (2 * H_A_KV + 2 * H_B) * HEAD_DIM
MEM_COLS = 2 * H_M * HEAD_DIM
QB_OFF = H_A * HEAD_DIM
QM_OFF = (H_A + H_B) * HEAD_DIM
KA_OFF = 0
VA_OFF = H_A_KV * HEAD_DIM
KB_OFF = 2 * H_A_KV * HEAD_DIM
VB_OFF = KB_OFF + H_B * HEAD_DIM

V7X_VMEM_LIMIT_BYTES = 62 * 1024 * 1024
HEAD_GROUP = 4
FF_TILE = 1024
FFN_ROW_TILE = 1024
V7X_LANES = 128
V7X_BF16_SUBLANES = 16
V7X_MXU_DIM = 256
CAST_ROW_TILE = 256
ROW_TILE = 512
PROMPT_Q_TILE = 256
MASK_VALUE = -1e30

_NT_DIMS = (((1,), (1,)), ((), ()))


def _row_tile(rows, preferred):
    tile = min(rows, preferred)
    while rows % tile:
        tile //= 2
    return tile


def _params(semantics):
    return pltpu.CompilerParams(dimension_semantics=semantics,
                                vmem_limit_bytes=V7X_VMEM_LIMIT_BYTES)


def _rms_rows(x, g):
    ms = jnp.mean(x * x, axis=-1, keepdims=True)
    return x * lax.rsqrt(ms + EPS) * g


def _layer_spec(shape, index_map, **kwargs):
    return pl.BlockSpec((None,) + tuple(shape), index_map, **kwargs)


def _ragged_step(n_ff, last_cols):
    return 1 if (last_cols != FF_TILE and n_ff > 2) else n_ff - 1


def _ff_block(j, n_ff, last_cols):
    ragged = _ragged_step(n_ff, last_cols)
    if ragged == n_ff - 1:
        return j
    return jnp.where(j == ragged, n_ff - 1, jnp.where(j < ragged, j, j - 1))


def _ffn_kernel(x_hbm, g_ref, wg_ref, wu_ref, wo_ref, o_ref, xn_ref, x_ref, x_sem, *, n_ff, last_cols):
    i = pl.program_id(0)
    j = pl.program_id(1)
    n_tiles = pl.num_programs(0)
    tm = x_ref.shape[0]
    ragged = _ragged_step(n_ff, last_cols)

    def x_copy(tile):
        rows = pl.ds(pl.multiple_of(tile * tm, tm), tm)
        return pltpu.make_async_copy(x_hbm.at[rows, :], x_ref, x_sem)

    @pl.when((j == 0) if n_ff == 1 else (i == 0) & (j == 0))
    def _():
        x_copy(i).start()

    @pl.when(j == 0)
    def _():
        x_copy(i).wait()

    if n_ff > 1:
        @pl.when((j == 1) & (i + 1 < n_tiles))
        def _():
            x_copy(i + 1).start()

    def accumulate_onto(base_ref, cols):
        xn = xn_ref[...]
        gate = jnp.dot(xn, wg_ref[:, :cols], preferred_element_type=jnp.float32)
        up = jnp.dot(xn, wu_ref[:, :cols], preferred_element_type=jnp.float32)
        act = (gate * jax.nn.sigmoid(gate) * up * 0.5).astype(jnp.bfloat16)
        o_ref[...] = base_ref[...] + jnp.dot(act, wo_ref[:cols, :], preferred_element_type=jnp.float32)

    def normalise():
        xn_ref[...] = _rms_rows(x_ref[...], g_ref[...]).astype(xn_ref.dtype)

    @pl.when(j == 0)
    def _():
        normalise()
        accumulate_onto(x_ref, FF_TILE if n_ff > 1 else last_cols)

    if last_cols == FF_TILE:
        @pl.when(j != 0)
        def _():
            accumulate_onto(o_ref, FF_TILE)
    elif n_ff > 1:
        @pl.when((j != 0) & (j != ragged))
        def _():
            accumulate_onto(o_ref, FF_TILE)

        @pl.when(j == ragged)
        def _():
            accumulate_onto(o_ref, last_cols)


def _ffn(x, g, weights, layer):
    wg, wu, wo = weights
    rows, d = x.shape
    f = wo.shape[0]
    n_ff = pl.cdiv(f, FF_TILE)
    tm = _row_tile(rows, FFN_ROW_TILE)
    last_cols = f - (n_ff - 1) * FF_TILE
    block = functools.partial(_ff_block, n_ff=n_ff, last_cols=last_cols)
    return pl.pallas_call(
        functools.partial(_ffn_kernel, n_ff=n_ff, last_cols=last_cols),
        grid=(rows // tm, n_ff),
        in_specs=[
            pl.BlockSpec(memory_space=pl.ANY),
            _layer_spec((1, d), lambda i, j: (layer, 0, 0)),
            pl.BlockSpec((d, FF_TILE), lambda i, j: (0, block(j))),
            pl.BlockSpec((d, FF_TILE), lambda i, j: (0, block(j))),
            pl.BlockSpec((FF_TILE, d), lambda i, j: (block(j), 0)),
        ],
        out_specs=pl.BlockSpec((tm, d), lambda i, j: (i, 0)),
        out_shape=jax.ShapeDtypeStruct((rows, d), jnp.float32),
        scratch_shapes=[pltpu.VMEM((tm, d), jnp.bfloat16), pltpu.VMEM((tm, d), jnp.float32),
                        pltpu.SemaphoreType.DMA(())],
        compiler_params=_params(("arbitrary", "arbitrary")),
        name="ffn",
    )(x, g, wg, wu, wo)


def _cast_kernel(w_ref, o_ref):
    o_ref[...] = w_ref[...].astype(o_ref.dtype)


def _ffn_weights_bf16(w_i, w_o, layer):
    _, d, two_f = w_i.shape
    f = two_f // 2
    assert f % V7X_LANES == 0
    rt = _row_tile(d, CAST_ROW_TILE)

    def half(part):
        return pl.pallas_call(
            _cast_kernel,
            grid=(d // rt,),
            in_specs=[_layer_spec((rt, f), lambda r: (layer, r, part))],
            out_specs=pl.BlockSpec((rt, f), lambda r: (r, 0)),
            out_shape=jax.ShapeDtypeStruct((d, f), jnp.bfloat16),
            compiler_params=_params(("parallel",)),
            name="cast_cols",
        )(w_i)

    return half(0), half(1), w_o[layer].astype(jnp.bfloat16)


def _head_plan():
    plan = []
    for h in range(H_A):
        plan.append((0, True, True, h * HEAD_DIM))
    for h in range(H_A_KV):
        plan.append((1, True, False, KA_OFF + h * HEAD_DIM))
    for h in range(H_A_KV):
        plan.append((None, False, False, VA_OFF + h * HEAD_DIM))
    for h in range(H_B):
        plan.append((2, False, True, QB_OFF + h * HEAD_DIM))
    for h in range(H_B):
        plan.append((3, False, False, KB_OFF + h * HEAD_DIM))
    for h in range(H_B):
        plan.append((None, False, False, VB_OFF + h * HEAD_DIM))
    for h in range(H_M):
        plan.append((4, False, True, QM_OFF + h * HEAD_DIM))
    return plan


_PLAN = _head_plan()
IN_COLS = len(_PLAN) * HEAD_DIM


def _proj_kernel(x_ref, g_ref, w_ref, qkg_ref, cos_ref, sa_ref, sb_ref, *rest, n_cast):
    cast_in = rest[:n_cast]
    q_ref, kvb_ref, kvf_ref = rest[n_cast:n_cast + 3]
    cast_out = rest[n_cast + 3:]
    for src, dst in zip(cast_in, cast_out):
        dst[...] = src[...].astype(dst.dtype)

    xn = _rms_rows(x_ref[...], g_ref[...]).astype(jnp.bfloat16)
    cos = cos_ref[...]
    sa = sa_ref[...]
    sb = sb_ref[...]
    gw = HEAD_GROUP * HEAD_DIM
    groups = sorted(range(len(_PLAN) // HEAD_GROUP),
                    key=lambda g: all(_PLAN[g * HEAD_GROUP + hh][0] is None for hh in range(HEAD_GROUP)))
    for grp in groups:
        p = jnp.dot(xn, w_ref[:, grp * gw:(grp + 1) * gw], preferred_element_type=jnp.float32)
        for hh in range(HEAD_GROUP):
            gain, rotary, is_query, col = _PLAN[grp * HEAD_GROUP + hh]
            z = p[:, hh * HEAD_DIM:(hh + 1) * HEAD_DIM]
            if gain is not None:
                z = _rms_rows(z, qkg_ref[gain:gain + 1, :])
            if rotary:
                z = (z * cos + pltpu.roll(z, HEAD_DIM - ROT_DIM // 2, 1) * sa
                     + pltpu.roll(z, ROT_DIM // 2, 1) * sb)
            cols = slice(col, col + HEAD_DIM)
            if is_query:
                q_ref[:, cols] = (z * Q_SCALE).astype(q_ref.dtype)
            else:
                kvf_ref[:, cols] = z
                kvb_ref[:, cols] = z.astype(kvb_ref.dtype)


def _rope_tables(pos):
    half = ROT_DIM // 2
    inv_freq = ROPE_THETA ** (-jnp.arange(half, dtype=jnp.float32) / half)
    ang = pos.astype(jnp.float32)[:, None] * inv_freq[None, :]
    cos = jnp.cos(ang)
    sin = jnp.sin(ang)
    t = pos.shape[0]
    ones = jnp.ones((t, HEAD_DIM - ROT_DIM), jnp.float32)
    cos_t = jnp.concatenate([cos, cos, ones], axis=1)
    sa_t = jnp.concatenate([-sin, jnp.zeros((t, HEAD_DIM - half), jnp.float32)], axis=1)
    sb_t = jnp.concatenate([jnp.zeros((t, half), jnp.float32), sin,
                            jnp.zeros((t, HEAD_DIM - ROT_DIM), jnp.float32)], axis=1)
    return cos_t, sa_t, sb_t


def _ffn_cast_fits(rows, d, f):
    n_steps = rows // _row_tile(rows, ROW_TILE)
    return (all(r % n_steps == 0 and (r // n_steps) % V7X_BF16_SUBLANES == 0 for r in (d, Q_COLS))
            and f % V7X_LANES == 0 and n_steps >= f // V7X_LANES)


def _ffn_cast_streams(w_i, w_o, layer, n_steps):
    _, d, two_f = w_i.shape
    f = two_f // 2
    rd = d // n_steps
    last = f // V7X_LANES - 1
    in_specs = [
        _layer_spec((rd, f), lambda i: (layer, i, 0)),
        _layer_spec((rd, f), lambda i: (layer, i, 1)),
        _layer_spec((V7X_LANES, d), lambda i: (layer, jnp.minimum(i, last), 0)),
    ]
    out_specs = [
        pl.BlockSpec((rd, f), lambda i: (i, 0)),
        pl.BlockSpec((rd, f), lambda i: (i, 0)),
        pl.BlockSpec((V7X_LANES, d), lambda i: (jnp.minimum(i, last), 0)),
    ]
    out_shapes = [jax.ShapeDtypeStruct((d, f), jnp.bfloat16)] * 2 + [
        jax.ShapeDtypeStruct((f, d), jnp.bfloat16)]
    return [w_i, w_i, w_o], in_specs, out_specs, out_shapes


def _row_cast_stream(w, layer, n_steps):
    _, r, c = w.shape
    rr = r // n_steps
    assert r % n_steps == 0 and rr % V7X_BF16_SUBLANES == 0
    return ([w], [_layer_spec((rr, c), lambda i: (layer, i, 0))],
            [pl.BlockSpec((rr, c), lambda i: (i, 0))], [jax.ShapeDtypeStruct((r, c), jnp.bfloat16)])


def _proj(x, g, w_in, qk_gain, tables, seq_len, layer, casts=(), row_casts=()):
    rows, d = x.shape
    tm = _row_tile(rows, ROW_TILE)
    cast_args, cast_in, cast_out, cast_shapes = [], [], [], []
    streams = [_ffn_cast_streams(w_i, w_o, cast_layer, rows // tm) for w_i, w_o, cast_layer in casts]
    streams += [_row_cast_stream(w, cast_layer, rows // tm) for w, cast_layer in row_casts]
    for a, i_s, o_s, shp in streams:
        cast_args += a
        cast_in += i_s
        cast_out += o_s
        cast_shapes += shp
    if tm <= seq_len:
        assert seq_len % tm == 0
        per_seq = seq_len // tm
        tab_map = lambda i: (i % per_seq, 0)
    else:
        assert tm % seq_len == 0
        tables = tuple(jnp.tile(t, (tm // seq_len, 1)) for t in tables)
        tab_map = lambda i: (0, 0)
    tab_spec = pl.BlockSpec((tm, HEAD_DIM), tab_map)
    return pl.pallas_call(
        functools.partial(_proj_kernel, n_cast=len(cast_args)),
        grid=(rows // tm,),
        in_specs=[
            pl.BlockSpec((tm, d), lambda i: (i, 0)),
            _layer_spec((1, d), lambda i: (layer, 0, 0)),
            pl.BlockSpec((d, IN_COLS), lambda i: (0, 0), pipeline_mode=pl.Buffered(1)),
            _layer_spec(qk_gain.shape[1:], lambda i: (layer, 0, 0)),
            tab_spec, tab_spec, tab_spec,
        ] + cast_in,
        out_specs=[
            pl.BlockSpec((tm, Q_COLS), lambda i: (i, 0)),
            pl.BlockSpec((tm, KV_COLS), lambda i: (i, 0)),
            pl.BlockSpec((tm, KV_COLS), lambda i: (i, 0)),
        ] + cast_out,
        out_shape=[
            jax.ShapeDtypeStruct((rows, Q_COLS), jnp.bfloat16),
            jax.ShapeDtypeStruct((rows, KV_COLS), jnp.bfloat16),
            jax.ShapeDtypeStruct((rows, KV_COLS), jnp.float32),
        ] + cast_shapes,
        compiler_params=_params(("arbitrary",)),
        name="proj",
    )(x, g, w_in, qk_gain, *tables, *cast_args)


def _memkv_kernel(x_ref, g_ref, w_ref, qkg_ref, of_ref, ob_ref):
    xn = _rms_rows(x_ref[...], g_ref[...]).astype(jnp.bfloat16)
    gw = H_M * HEAD_DIM
    for part in range(2):
        p = jnp.dot(xn, w_ref[:, part * gw:(part + 1) * gw], preferred_element_type=jnp.float32)
        for h in range(H_M):
            z = p[:, h * HEAD_DIM:(h + 1) * HEAD_DIM]
            if part == 0:
                z = _rms_rows(z, qkg_ref[5:6, :])
            cols = slice(part * gw + h * HEAD_DIM, part * gw + (h + 1) * HEAD_DIM)
            of_ref[:, cols] = z
            ob_ref[:, cols] = z.astype(ob_ref.dtype)


def _memkv(mem, g, w_kv, qk_gain, layer):
    rows, d = mem.shape
    tm = _row_tile(rows, ROW_TILE)
    return pl.pallas_call(
        _memkv_kernel,
        grid=(rows // tm,),
        in_specs=[
            pl.BlockSpec((tm, d), lambda i: (i, 0)),
            _layer_spec((1, d), lambda i: (layer, 0, 0)),
            pl.BlockSpec((d, MEM_COLS), lambda i: (0, 0)),
            _layer_spec(qk_gain.shape[1:], lambda i: (layer, 0, 0)),
        ],
        out_specs=[pl.BlockSpec((tm, MEM_COLS), lambda i: (i, 0))] * 2,
        out_shape=[jax.ShapeDtypeStruct((rows, MEM_COLS), jnp.float32),
                   jax.ShapeDtypeStruct((rows, MEM_COLS), jnp.bfloat16)],
        compiler_params=_params(("parallel",)),
        name="memkv",
    )(mem, g, w_kv, qk_gain)


def _attn_kernel(sink_ref, q_ref, kv_ref, mkv_ref, ba_ref, bb_ref, x_ref, w_ref, y_ref, o_ref, *,
                 layer, q_tile, base, sub_a, pieces_b):
    tile = pl.program_id(1)
    row0 = pl.multiple_of(tile * q_tile, q_tile)
    n_sub = q_tile // sub_a
    n_var_a = ba_ref.shape[0]
    group = H_A // H_A_KV

    def piece_rows(off, size):
        start = base + row0 + off
        if base + off < 0:
            start = jnp.maximum(start, 0)
        return pl.ds(pl.multiple_of(start, math.gcd(q_tile, abs(off), base)), size)

    def window_a(u):
        start = base + row0 + u * sub_a - WINDOW
        if base + u * sub_a - WINDOW < 0:
            start = jnp.maximum(start, 0)
        return pl.ds(pl.multiple_of(start, math.gcd(sub_a, base)), WINDOW + sub_a)

    def bias_a(u):
        return ba_ref[jnp.minimum(tile * n_sub + u, n_var_a - 1)]

    jobs = []
    for g in range(H_A_KV):
        heads = range(g * group, (g + 1) * group)
        sinks = [sink_ref[layer, h] * LOG2E for h in heads]
        for u in range(n_sub):
            jobs.append((slice(u * sub_a, (u + 1) * sub_a), [h * HEAD_DIM for h in heads], kv_ref,
                         [window_a(u)], KA_OFF + g * HEAD_DIM, VA_OFF + g * HEAD_DIM,
                         functools.partial(bias_a, u), sinks))
    rows_b = [piece_rows(off, size) for off, size in pieces_b]
    for h in range(H_B):
        jobs.append((slice(0, q_tile), [QB_OFF + h * HEAD_DIM], kv_ref, rows_b, KB_OFF + h * HEAD_DIM,
                     VB_OFF + h * HEAD_DIM, functools.partial(lambda hh: bb_ref[0, hh], h), None))
    rows_m = [pl.ds(0, mkv_ref.shape[1])]
    for h in range(H_M):
        jobs.append((slice(0, q_tile), [QM_OFF + h * HEAD_DIM], mkv_ref, rows_m, h * HEAD_DIM,
                     (H_M + h) * HEAD_DIM, None, None))

    scores = []
    for q_rows, q_cols, k_ref, rows_list, k_col, _, _, _ in jobs:
        qs = [q_ref[0, q_rows, c:c + HEAD_DIM] for c in q_cols]
        q = qs[0] if len(qs) == 1 else jnp.concatenate(qs, axis=0)
        parts = [lax.dot_general(q, k_ref[0, rows, k_col:k_col + HEAD_DIM], _NT_DIMS,
                                 preferred_element_type=jnp.float32) for rows in rows_list]
        scores.append(parts[0] if len(parts) == 1 else jnp.concatenate(parts, axis=1))

    probs = []
    dens = []
    for (q_rows, q_cols, _, _, _, _, bias, sinks), s_all in zip(jobs, scores):
        n_rows = q_rows.stop - q_rows.start
        bias_val = None if bias is None else bias()
        es = []
        ds = []
        for i in range(len(q_cols)):
            s = s_all[i * n_rows:(i + 1) * n_rows]
            if bias_val is not None:
                s = s + bias_val
            m = jnp.max(s, axis=-1, keepdims=True)
            if sinks is not None:
                m = jnp.maximum(m, sinks[i])
            e = jnp.exp2(s - m)
            den = None if sinks is None else jnp.exp2(sinks[i] - m)
            es.append(e.astype(jnp.bfloat16))
            ds.append(den)
        probs.append(es[0] if len(es) == 1 else jnp.concatenate(es, axis=0))
        dens.append(ds)

    def weighted_values(job, e, ds):
        q_rows, q_cols, k_ref, rows_list, _, v_col, _, _ = job
        n_rows = q_rows.stop - q_rows.start
        acc = None
        lo = 0
        for rows in rows_list:
            v_ones = jnp.concatenate([k_ref[0, rows, v_col:v_col + HEAD_DIM],
                                      jnp.ones((rows.size, HEAD_DIM), jnp.bfloat16)], axis=1)
            pv = jnp.dot(e[:, lo:lo + rows.size], v_ones, preferred_element_type=jnp.float32)
            acc = pv if acc is None else acc + pv
            lo += rows.size
        for i, c in enumerate(q_cols):
            blk = acc[i * n_rows:(i + 1) * n_rows]
            den = blk[:, HEAD_DIM:] if ds[i] is None else blk[:, HEAD_DIM:] + ds[i]
            o = blk[:, :HEAD_DIM] / den
            o_ref[q_rows, c:c + HEAD_DIM] = o.astype(o_ref.dtype)

    for job, e, ds in zip(jobs, probs, dens):
        weighted_values(job, e, ds)

    y = x_ref[0]
    for c in range(0, Q_COLS, V7X_MXU_DIM):
        y = y + jnp.dot(o_ref[:, c:c + V7X_MXU_DIM], w_ref[c:c + V7X_MXU_DIM, :],
                        preferred_element_type=jnp.float32)
    y_ref[0] = y


def _window_pieces(past_rows, q_tile, base):
    size = past_rows if base >= past_rows else math.gcd(past_rows, q_tile)
    pieces = [(-past_rows + k * size, size) for k in range(past_rows // size)]
    return pieces + [(0, q_tile)]


def _leading_variants(masks):
    n_var = len(masks)
    while n_var > 1 and np.array_equal(masks[n_var - 2], masks[-1]):
        n_var -= 1
    return np.stack(masks[:n_var])


def _band_masks(q_tile, past_rows, n_past_chunks, n_tiles, first_real, new_rows):
    win = past_rows + q_tile
    r = np.arange(q_tile)[:, None] // CHUNK
    w = np.arange(win)[None, :]
    band = (w // CHUNK >= r) & (w // CHUNK <= r + n_past_chunks)
    masks = []
    for i in range(n_tiles):
        row = i * q_tile + w - past_rows
        masks.append(band & (row >= first_real) & (row < new_rows))
    return _leading_variants(masks)


def _window_masks_a(sub, n_sub_tiles, base, first_real, new_rows):
    masks = []
    for s in range(n_sub_tiles):
        q_row = s * sub + np.arange(sub)[:, None]
        k_row = max(s * sub - WINDOW, -base) + np.arange(WINDOW + sub)[None, :]
        q_chunk = q_row // CHUNK
        k_chunk = np.floor_divide(k_row, CHUNK)
        masks.append((k_chunk >= q_chunk - WIN_CHUNKS) & (k_chunk <= q_chunk)
                     & (k_row >= first_real) & (k_row < new_rows))
    return _leading_variants(masks)


def _rel_bias_tiles(rel_tab, q_tile):
    win = BAND_ROWS + q_tile
    n = win + q_tile
    k = np.arange(n)
    dist = np.where(k < win, BAND_ROWS - k, BAND_ROWS + n - k)
    seq = rel_tab[:, :, np.clip(dist, -REL_CLIP, REL_CLIP) + REL_CLIP].astype(jnp.float32) * LOG2E
    flat = jnp.tile(seq, (1, 1, q_tile))[:, :, :q_tile * (n - 1)]
    return flat.reshape(rel_tab.shape[0], H_B, q_tile, n - 1)[..., :win]


def _attn_biases(rel_tab, q_tile, n_tiles, base, past_a, past_b, new_rows):
    sub = min(q_tile, WINDOW)
    mask_a = _window_masks_a(sub, n_tiles * (q_tile // sub), base, -past_a, new_rows)
    mask_b = _band_masks(q_tile, BAND_ROWS, BAND_CHUNKS, n_tiles, -past_b, new_rows)
    bias_a = jnp.asarray(np.where(mask_a, 0.0, MASK_VALUE), jnp.float32)
    rel = _rel_bias_tiles(rel_tab, q_tile)
    bias_b = jnp.where(jnp.asarray(mask_b)[None, :, None], rel[:, None], MASK_VALUE)
    return bias_a, bias_b


def _mixer(x, q, kv, mkv, sinks, bias_a, bias_b, w_out, q_tile, base, layer):
    b, tq, d = x.shape
    assert kv.shape[1] == base + tq and q.shape[1] == tq
    n_b = bias_b.shape[1]
    win_b = BAND_ROWS + q_tile
    kern = functools.partial(_attn_kernel, layer=layer, q_tile=q_tile, base=base,
                             sub_a=bias_a.shape[1],
                             pieces_b=_window_pieces(BAND_ROWS, q_tile, base))
    return pl.pallas_call(
        kern,
        grid=(b, tq // q_tile),
        in_specs=[
            pl.BlockSpec(memory_space=pltpu.SMEM),
            pl.BlockSpec((1, q_tile, Q_COLS), lambda bi, i: (bi, i, 0)),
            pl.BlockSpec((1,) + kv.shape[1:], lambda bi, i: (bi, 0, 0)),
            pl.BlockSpec((1,) + mkv.shape[1:], lambda bi, i: (bi, 0, 0)),
            pl.BlockSpec(bias_a.shape, lambda bi, i: (0, 0, 0)),
            _layer_spec((1, H_B, q_tile, win_b),
                        lambda bi, i: (layer, jnp.minimum(i, n_b - 1), 0, 0, 0)),
            pl.BlockSpec((1, q_tile, d), lambda bi, i: (bi, i, 0)),
            pl.BlockSpec((Q_COLS, d), lambda bi, i: (0, 0), pipeline_mode=pl.Buffered(1)),
        ],
        out_specs=pl.BlockSpec((1, q_tile, d), lambda bi, i: (bi, i, 0)),
        out_shape=jax.ShapeDtypeStruct((b, tq, d), jnp.float32),
        scratch_shapes=[pltpu.VMEM((q_tile, Q_COLS), jnp.bfloat16)],
        compiler_params=_params(("parallel", "arbitrary")),
        name="mixer",
    )(sinks, q, kv, mkv, bias_a, bias_b, x, w_out)


def kernel(x_prompt, x_sample, cache_a_k, cache_a_v, cache_b_k, cache_b_v, cache_mem_k, cache_mem_v,
           mem_prompt, norm_ff1, w_ff1_in, w_ff1_out, norm_mix, w_in, qk_gain, attn_sinks, rel_bias,
           norm_mem, w_mem_kv, w_out, norm_ff2, w_ff2_in, w_ff2_out):
    b_p, t_p, d = x_prompt.shape
    b_s, t_s, _ = x_sample.shape
    depth = w_in.shape[0]
    n_mem = mem_prompt.shape[1]
    bf16 = jnp.bfloat16
    assert t_p % PROMPT_Q_TILE == 0 and t_s <= CHUNK

    ride = _ffn_cast_fits(b_p * t_p, d, w_ff1_out.shape[1])
    mix_stacks = (w_in, w_out, w_mem_kv)
    ff1_w = _ffn_weights_bf16(w_ff1_in, w_ff1_out, 0)
    mix_w = [w[0].astype(bf16) for w in mix_stacks]
    g_ff1 = norm_ff1[:, None, :]
    g_ff2 = norm_ff2[:, None, :]
    g_mix = norm_mix[:, None, :]
    g_mem = norm_mem[:, None, :]

    tab_p = _rope_tables(jnp.arange(t_p, dtype=jnp.int32))
    tab_s = _rope_tables(PAST_LEN + jnp.arange(t_s, dtype=jnp.int32))
    keep_a = min(WINDOW, t_p)
    keep_b = min(BAND_ROWS, t_p)
    la = cache_a_k.shape[2]
    lb = cache_b_k.shape[2]

    bias_a_p, bias_b_p = _attn_biases(rel_bias, PROMPT_Q_TILE, t_p // PROMPT_Q_TILE, 0, 0, 0, t_p)
    bias_a_s, bias_b_s = _attn_biases(rel_bias, CHUNK, 1, BAND_ROWS, la, lb, t_s)

    def heads_flat(c):
        return c.reshape(c.shape[:3] + (-1,)).astype(bf16)

    def front(c, rows):
        return jnp.pad(c, ((0, 0), (0, 0), (rows - c.shape[2], 0), (0, 0)))

    past_kv = jnp.concatenate(
        [front(heads_flat(cache_a_k), BAND_ROWS), front(heads_flat(cache_a_v), BAND_ROWS),
         front(heads_flat(cache_b_k), BAND_ROWS), front(heads_flat(cache_b_v), BAND_ROWS)], axis=3)
    mkv_s = jnp.concatenate([heads_flat(cache_mem_k), heads_flat(cache_mem_v)], axis=3)

    xp = x_prompt.reshape(b_p * t_p, d)
    xs = x_sample.reshape(b_s * t_s, d)
    mem_flat = mem_prompt.reshape(b_p * n_mem, d)
    outs = [[] for _ in range(10)]
    for l in range(depth):
        w_in_l, w_out_l, w_mem_l = mix_w
        mkv_f, mkv_b = _memkv(mem_flat, g_mem, w_mem_l, qk_gain, l)
        xp = _ffn(xp, g_ff1, ff1_w, l)
        more = l + 1 < depth
        casts = [(w_ff2_in, w_ff2_out, l)] + ([(w_ff1_in, w_ff1_out, l + 1)] if more else [])
        row_casts = [(w, l + 1) for w in mix_stacks] if more else []
        if ride:
            q, kvb, kvf, *cast_w = _proj(xp, g_mix, w_in_l, qk_gain, tab_p, t_p, l, casts, row_casts)
            ff2_w, next_ff1_w, next_mix_w = cast_w[:3], cast_w[3:6], cast_w[6:]
        else:
            q, kvb, kvf = _proj(xp, g_mix, w_in_l, qk_gain, tab_p, t_p, l)
            ff2_w = _ffn_weights_bf16(w_ff2_in, w_ff2_out, l)
            next_ff1_w = _ffn_weights_bf16(w_ff1_in, w_ff1_out, l + 1) if more else None
            next_mix_w = [w[l + 1].astype(bf16) for w in mix_stacks] if more else None
        xp = _mixer(xp.reshape(b_p, t_p, d), q.reshape(b_p, t_p, Q_COLS),
                    kvb.reshape(b_p, t_p, KV_COLS), mkv_b.reshape(b_p, n_mem, MEM_COLS),
                    attn_sinks, bias_a_p, bias_b_p, w_out_l, PROMPT_Q_TILE, 0, l).reshape(b_p * t_p, d)
        xp = _ffn(xp, g_ff2, ff2_w, l)
        kvf = kvf.reshape(b_p, t_p, KV_COLS)
        outs[0].append(kvf[:, t_p - keep_a:, KA_OFF:VA_OFF])
        outs[1].append(kvf[:, t_p - keep_a:, VA_OFF:KB_OFF])
        outs[2].append(kvf[:, t_p - keep_b:, KB_OFF:VB_OFF])
        outs[3].append(kvf[:, t_p - keep_b:, VB_OFF:])
        outs[4].append(mkv_f[:, :H_M * HEAD_DIM])
        outs[5].append(mkv_f[:, H_M * HEAD_DIM:])

        xs = _ffn(xs, g_ff1, ff1_w, l)
        q, kvb, kvf = _proj(xs, g_mix, w_in_l, qk_gain, tab_s, t_s, l)
        tail = ((0, 0), (0, CHUNK - t_s), (0, 0))
        kv = jnp.concatenate([past_kv[l], jnp.pad(kvb.reshape(b_s, t_s, KV_COLS), tail)], axis=1)
        y = _mixer(jnp.pad(xs.reshape(b_s, t_s, d), tail), jnp.pad(q.reshape(b_s, t_s, Q_COLS), tail),
                   kv, mkv_s[l], attn_sinks, bias_a_s, bias_b_s, w_out_l, CHUNK, BAND_ROWS, l)
        xs = _ffn(y[:, :t_s].reshape(b_s * t_s, d), g_ff2, ff2_w, l)
        outs[6].append(kvf[:, KA_OFF:VA_OFF])
        outs[7].append(kvf[:, VA_OFF:KB_OFF])
        outs[8].append(kvf[:, KB_OFF:VB_OFF])
        outs[9].append(kvf[:, VB_OFF:])
        ff1_w, mix_w = next_ff1_w, next_mix_w

    def stacked_heads(per_layer, batch):
        z = jnp.stack(per_layer)
        return z.reshape(depth, batch, -1, z.shape[-1] // HEAD_DIM, HEAD_DIM)

    return ((xp.reshape(b_p, t_p, d), xs.reshape(b_s, t_s, d))
            + tuple(stacked_heads(o, b_p) for o in outs[:6])
            + tuple(stacked_heads(o, b_s) for o in outs[6:]))
```
